```python
import jax, jax.numpy as jnp
from jax import lax
import numpy as np

D_MODEL = 2048
BATCH = 2
SEQ = 16384
DEPTH = 1

ATTN_HEADS = 16
ATTN_KV_HEADS = 4
ATTN_HEAD_DIM = 64
ATTN_GROUP = ATTN_HEADS // ATTN_KV_HEADS
ATTN_WIDTH = ATTN_HEADS * ATTN_HEAD_DIM
KV_WIDTH = ATTN_KV_HEADS * ATTN_HEAD_DIM
WINDOW = 128
ATTN_BLOCK = 128
ROPE_THETA = 10000.0

HGRN_WIDTH = D_MODEL // 2
HGRN_EXPAND = 128
HGRN_HEADS = HGRN_WIDTH // HGRN_EXPAND
HGRN_KEY_DIM = HGRN_EXPAND
HGRN_VALUE_DIM = HGRN_WIDTH // HGRN_HEADS
HGRN_FORGET_WIDTH = HGRN_HEADS * HGRN_KEY_DIM
HGRN_CHUNK = 64

NORM_EPS = 1e-6

IN_SPLITS = (
    ATTN_WIDTH,
    KV_WIDTH,
    KV_WIDTH,
    ATTN_WIDTH,
    HGRN_FORGET_WIDTH,
    HGRN_FORGET_WIDTH,
    HGRN_WIDTH,
    HGRN_WIDTH,
    D_MODEL,
    D_MODEL,
)
IN_WIDTH = int(sum(IN_SPLITS))
SPLIT_POINTS = tuple(int(s) for s in np.cumsum(IN_SPLITS)[:-1])

kernel_name = "hybrid_swa_sink_hgrn2_gated_merge"


def rms_norm(x, gain):
    xf = x.astype(jnp.float32)
    y = xf * lax.rsqrt(jnp.mean(xf * xf, axis=-1, keepdims=True) + NORM_EPS)
    return (y * gain.astype(jnp.float32)).astype(x.dtype)


def rotary(x, positions):
    half = x.shape[-1] // 2
    inv_freq = ROPE_THETA ** (-jnp.arange(half, dtype=jnp.float32) / half)
    ang = positions.astype(jnp.float32)[..., None] * inv_freq
    cos = jnp.cos(ang)[:, :, None, :]
    sin = jnp.sin(ang)[:, :, None, :]
    xf = x.astype(jnp.float32)
    x1, x2 = xf[..., :half], xf[..., half:]
    return jnp.concatenate([x1 * cos - x2 * sin, x2 * cos + x1 * sin], axis=-1).astype(x.dtype)


def sliding_window_attention(q, k, v, sinks):
    B, T = q.shape[0], q.shape[1]
    nb = T // ATTN_BLOCK
    qb = q.reshape(B, nb, ATTN_BLOCK, ATTN_KV_HEADS, ATTN_GROUP, ATTN_HEAD_DIM)

    def with_prev(a):
        ab = a.reshape(B, nb, ATTN_BLOCK, ATTN_KV_HEADS, ATTN_HEAD_DIM)
        prev = jnp.pad(ab[:, :-1], ((0, 0), (1, 0), (0, 0), (0, 0), (0, 0)))
        return jnp.concatenate([prev, ab], axis=2)

    kb, vb = with_prev(k), with_prev(v)
    scores = jnp.einsum('bnqhgd,bnkhd->bnhgqk', qb, kb).astype(jnp.float32) * (ATTN_HEAD_DIM ** -0.5)
    q_pos = jnp.arange(ATTN_BLOCK)[:, None] + ATTN_BLOCK
    k_pos = jnp.arange(2 * ATTN_BLOCK)[None, :]
    rel = q_pos - k_pos
    band = (rel >= 0) & (rel < WINDOW)
    has_prev = (jnp.arange(nb) > 0)[:, None, None] | (k_pos >= ATTN_BLOCK)[None]
    mask = band[None] & has_prev
    scores = jnp.where(mask[None, :, None, None], scores, -jnp.inf)
    sink = jnp.broadcast_to(
        sinks.astype(jnp.float32).reshape(1, 1, ATTN_KV_HEADS, ATTN_GROUP, 1, 1),
        scores.shape[:-1] + (1,))
    probs = jax.nn.softmax(jnp.concatenate([scores, sink], axis=-1), axis=-1)[..., :-1]
    out = jnp.einsum('bnhgqk,bnkhd->bnqhgd', probs.astype(v.dtype), vb)
    return out.reshape(B, T, ATTN_WIDTH)


def hgrn2_chunked(q, k, v, log_f):
    B, T, H, dk = q.shape
    dv = v.shape[-1]
    nc = T // HGRN_CHUNK

    def to_chunks(a):
        return a.reshape(B, nc, HGRN_CHUNK, H, a.shape[-1]).transpose(1, 0, 2, 3, 4)

    qc, kc, vc = to_chunks(q), to_chunks(k), to_chunks(v)
    bc = jnp.cumsum(to_chunks(log_f), axis=2)
    causal = jnp.tril(jnp.ones((HGRN_CHUNK, HGRN_CHUNK), dtype=bool))[None, :, :, None, None]

    def step(S, inp):
        qt, kt, vt, bt = inp
        diff = bt[:, :, None] - bt[:, None, :]
        decay = jnp.exp(jnp.where(causal, diff, -jnp.inf))
        scores = jnp.einsum('bthd,bshd,btshd->bhts', qt, kt, decay)
        o_intra = jnp.einsum('bhts,bshv->bthv', scores, vt)
        o_inter = jnp.einsum('bthd,bhdv->bthv', qt * jnp.exp(bt), S)
        b_last = bt[:, -1]
        k_dec = kt * jnp.exp(b_last[:, None] - bt)
        S_new = jnp.exp(b_last)[..., None] * S + jnp.einsum('bshd,bshv->bhdv', k_dec, vt)
        return S_new, o_intra + o_inter

    S0 = jnp.zeros((B, H, dk, dv), dtype=jnp.float32)
    _, o = lax.scan(step, S0, (qc, kc, vc, bc))
    return o.transpose(1, 0, 2, 3, 4).reshape(B, T, H, dv)


def setup_inputs(seed: int = 0) -> dict:
    key = jax.random.key(seed)
    ks = jax.random.split(key, 11)
    f32 = jnp.float32
    x = jax.random.normal(ks[0], (BATCH, SEQ, D_MODEL), f32)
    positions = jnp.broadcast_to(jnp.arange(SEQ, dtype=jnp.int32), (BATCH, SEQ))
    norm_gain = 1.0 + 0.02 * jax.random.normal(ks[1], (DEPTH, D_MODEL), f32)
    w_in = jax.random.normal(ks[2], (DEPTH, D_MODEL, IN_WIDTH), f32) * D_MODEL ** -0.5
    attn_sinks = 0.5 * jax.random.normal(ks[3], (DEPTH, ATTN_HEADS), f32)
    hgrn_lower_bounds = 0.1 * jax.random.normal(ks[4], (DEPTH + 1, HGRN_FORGET_WIDTH), f32)
    hgrn_norm_gain = 1.0 + 0.02 * jax.random.normal(ks[5], (DEPTH, HGRN_HEADS, HGRN_VALUE_DIM), f32)
    w_attn_out = jax.random.normal(ks[6], (DEPTH, ATTN_WIDTH, D_MODEL), f32) * ATTN_WIDTH ** -0.5
    w_hgrn_out = jax.random.normal(ks[7], (DEPTH, HGRN_WIDTH, D_MODEL), f32) * HGRN_WIDTH ** -0.5
    w_o = jax.random.normal(ks[8], (DEPTH, D_MODEL, D_MODEL), f32) * D_MODEL ** -0.5
    final_norm_gain = 1.0 + 0.02 * jax.random.normal(ks[9], (D_MODEL,), f32)
    return {"x": x, "positions": positions, "norm_gain": norm_gain, "w_in": w_in,
            "attn_sinks": attn_sinks, "hgrn_lower_bounds": hgrn_lower_bounds,
            "hgrn_norm_gain": hgrn_norm_gain, "w_attn_out": w_attn_out,
            "w_hgrn_out": w_hgrn_out, "w_o": w_o, "final_norm_gain": final_norm_gain}


def reference(x, positions, norm_gain, w_in, attn_sinks, hgrn_lower_bounds, hgrn_norm_gain,
              w_attn_out, w_hgrn_out, w_o, final_norm_gain):
    B, T = x.shape[0], x.shape[1]
    lower_bounds = jnp.cumsum(jax.nn.softmax(hgrn_lower_bounds.astype(jnp.float32), axis=0), axis=0)
    for l in range(DEPTH):
        h = rms_norm(x, norm_gain[l])
        proj = h @ w_in[l]
        (aq, ak, av, a_gate, hq, hf, hi, h_gate, m_attn, m_hgrn) = jnp.split(proj, SPLIT_POINTS, axis=-1)

        aq = rotary(aq.reshape(B, T, ATTN_HEADS, ATTN_HEAD_DIM), positions)
        ak = rotary(ak.reshape(B, T, ATTN_KV_HEADS, ATTN_HEAD_DIM), positions)
        av = av.reshape(B, T, ATTN_KV_HEADS, ATTN_HEAD_DIM)
        attn = sliding_window_attention(aq, ak, av, attn_sinks[l])
        y_attn = (attn * jax.nn.silu(a_gate)) @ w_attn_out[l]

        lb = lower_bounds[l]
        hf32 = hf.astype(jnp.float32)
        forget = lb + (1.0 - lb) * jax.nn.sigmoid(hf32)
        log_f = jnp.log(forget)
        k_in = (1.0 - lb) * jax.nn.sigmoid(-hf32)
        q_r = jax.nn.silu(hq.astype(jnp.float32)) * (HGRN_KEY_DIM ** -0.5)
        heads_k = (B, T, HGRN_HEADS, HGRN_KEY_DIM)
        o = hgrn2_chunked(q_r.reshape(heads_k), k_in.reshape(heads_k),
                          hi.astype(jnp.float32).reshape(B, T, HGRN_HEADS, HGRN_VALUE_DIM),
                          log_f.reshape(heads_k))
        o = rms_norm(o, hgrn_norm_gain[l]).reshape(B, T, HGRN_WIDTH).astype(x.dtype)
        y_hgrn = (o * jax.nn.silu(h_gate)) @ w_hgrn_out[l]

        merged = jax.nn.sigmoid(m_attn) * y_attn + jax.nn.sigmoid(m_hgrn) * y_hgrn
        x = x + merged @ w_o[l]
    return rms_norm(x, final_norm_gain)
```

```python
import functools

import numpy as np
import jax
import jax.numpy as jnp
from jax import lax
from jax.experimental import pallas as pl
from jax.experimental.pallas import tpu as pltpu

F32 = jnp.float32
BF16 = jnp.bfloat16

ATTN_HEADS = 16
ATTN_KV_HEADS = 4
ATTN_HEAD_DIM = 64
ATTN_GROUP = ATTN_HEADS // ATTN_KV_HEADS
ATTN_WIDTH = ATTN_HEADS * ATTN_HEAD_DIM
KV_WIDTH = ATTN_KV_HEADS * ATTN_HEAD_DIM
WINDOW = 128
ROPE_THETA = 10000.0
HGRN_HEADS = 8
HGRN_KEY_DIM = 128
HGRN_VALUE_DIM = 128
HGRN_WIDTH = HGRN_HEADS * HGRN_VALUE_DIM
NORM_EPS = 1e-6

LANES = 128
VMEM_LIMIT = 56 * 1024 * 1024

_ORIG_SEGMENTS = (("aq", 1024), ("ak", 256), ("av", 256), ("ag", 1024), ("hq", 1024),
                  ("hf", 1024), ("hi", 1024), ("hg", 1024), ("ma", 2048), ("mh", 2048))
_PROJ_ORDER = ("aq", "ag", "hq", "hf", "hi", "hg", "ma", "mh", "ak", "av")


def _segment_table():
    orig, off = {}, 0
    for name, w in _ORIG_SEGMENTS:
        orig[name] = (off, w)
        off += w
    new, off = {}, 0
    for name in _PROJ_ORDER:
        new[name] = (off, orig[name][1])
        off += orig[name][1]
    return orig, new, off


_ORIG_OFF, _NEW_OFF, IN_WIDTH = _segment_table()


def _col_block(name, width):
    off = _NEW_OFF[name][0]
    assert off % width == 0
    return off // width


def _inproj_kernel(x_ref, g_ref, w_ref, o_ref, h_ref, *, row_chunk):
    @pl.when(pl.program_id(1) == 0)
    def _():
        def body(r, carry):
            rows = pl.ds(pl.multiple_of(r * row_chunk, row_chunk), row_chunk)
            x = x_ref[rows, :]
            ms = jnp.mean(x * x, axis=-1, keepdims=True)
            h_ref[rows, :] = ((x * lax.rsqrt(ms + NORM_EPS)) * g_ref[...]).astype(BF16)
            return carry
        lax.fori_loop(0, x_ref.shape[0] // row_chunk, body, 0)

    o_ref[...] = jnp.dot(h_ref[...], w_ref[...], preferred_element_type=F32).astype(o_ref.dtype)


def _in_projection(x2d, gain, w_bf16, *, tm=1024, tn=1536):
    n_tok, d = x2d.shape
    n_out = w_bf16.shape[1]
    assert n_tok % tm == 0 and n_out % tn == 0
    return pl.pallas_call(
        functools.partial(_inproj_kernel, row_chunk=128),
        grid=(n_tok // tm, n_out // tn),
        in_specs=[pl.BlockSpec((tm, d), lambda i, j: (i, 0)),
                  pl.BlockSpec((1, d), lambda i, j: (0, 0)),
                  pl.BlockSpec((d, tn), lambda i, j: (0, j))],
        out_specs=pl.BlockSpec((tm, tn), lambda i, j: (i, j)),
        out_shape=jax.ShapeDtypeStruct((n_tok, n_out), BF16),
        scratch_shapes=[pltpu.VMEM((tm, d), BF16)],
        compiler_params=pltpu.CompilerParams(
            dimension_semantics=("arbitrary", "arbitrary"), vmem_limit_bytes=VMEM_LIMIT),
        name="in_projection",
    )(x2d, gain.reshape(1, d), w_bf16)


def _rope_tables(pos_col, inv_freq_row):
    ang = pos_col.astype(F32) * inv_freq_row
    cos, sin = jnp.cos(ang), jnp.sin(ang)
    lane = lax.broadcasted_iota(jnp.int32, ang.shape, 1)
    first = (lane % ATTN_HEAD_DIM) < (ATTN_HEAD_DIM // 2)
    return cos, jnp.where(first, -sin, 0.0), jnp.where(first, 0.0, sin)


def _rope(x, cos, sin_first, sin_second):
    half = ATTN_HEAD_DIM // 2
    from_right = pltpu.roll(x, LANES - half, axis=1)
    from_left = pltpu.roll(x, half, axis=1)
    return x * cos + from_right * sin_first + from_left * sin_second


def _attn_kernel(sink_ref, invf_ref, pos_ref, ppos_ref, q_ref, g_ref, k_ref, v_ref, pk_ref, pv_ref, o_ref,
                 *, tq):
    blk = WINDOW
    nsub = tq // blk
    first_block = pl.program_id(1) == 0
    invf = invf_ref[...]
    cos_c, sf_c, ss_c = _rope_tables(pos_ref[...], invf)
    cos_p, sf_p, ss_p = _rope_tables(ppos_ref[...], invf)

    lane = lax.broadcasted_iota(jnp.int32, (1, LANES), 1)
    lo_lane = lane < ATTN_HEAD_DIM

    n_slab = KV_WIDTH // LANES
    k_lo, k_hi, v_lo, v_hi = [], [], [], []
    for s in range(n_slab):
        cols = slice(s * LANES, (s + 1) * LANES)
        kc = _rope(k_ref[:, cols].astype(F32), cos_c, sf_c, ss_c)
        kp = _rope(pk_ref[:, cols].astype(F32), cos_p, sf_p, ss_p)
        kk = jnp.concatenate([kp, kc], axis=0)
        vv = jnp.concatenate([pv_ref[:, cols], v_ref[:, cols]], axis=0).astype(F32)
        kk_sw = pltpu.roll(kk, ATTN_HEAD_DIM, axis=1)
        vv_sw = pltpu.roll(vv, ATTN_HEAD_DIM, axis=1)
        k_lo += [jnp.where(lo_lane, kk, 0.0).astype(BF16), jnp.where(lo_lane, kk_sw, 0.0).astype(BF16)]
        k_hi += [jnp.where(lo_lane, 0.0, kk_sw).astype(BF16), jnp.where(lo_lane, 0.0, kk).astype(BF16)]
        v_lo += [jnp.where(lo_lane, vv, 0.0).astype(BF16), jnp.where(lo_lane, vv_sw, 0.0).astype(BF16)]
        v_hi += [jnp.where(lo_lane, 0.0, vv_sw).astype(BF16), jnp.where(lo_lane, 0.0, vv).astype(BF16)]

    r_idx = lax.broadcasted_iota(jnp.int32, (blk, 2 * blk), 0)
    c_idx = lax.broadcasted_iota(jnp.int32, (blk, 2 * blk), 1)
    band = (c_idx > r_idx) & (c_idx <= r_idx + blk)
    scale = ATTN_HEAD_DIM ** -0.5
    nt = (((1,), (1,)), ((), ()))

    for sub in range(nsub):
        rows = slice(sub * blk, (sub + 1) * blk)
        keys = slice(sub * blk, sub * blk + 2 * blk)
        if sub == 0:
            mask = band & (c_idx >= jnp.where(first_block, blk, 0))
        else:
            mask = band
        for pair in range(ATTN_HEADS // 2):
            cols = slice(pair * LANES, (pair + 1) * LANES)
            g = (2 * pair) // ATTN_GROUP
            q2 = _rope(q_ref[rows, cols].astype(F32), cos_c[rows], sf_c[rows], ss_c[rows])
            q2 = (q2 * scale).astype(BF16)
            acc = jnp.zeros((blk, LANES), F32)
            inv = []
            for which, (kpad, vpad) in enumerate(((k_lo[g], v_lo[g]), (k_hi[g], v_hi[g]))):
                s = lax.dot_general(q2, kpad[keys], nt, preferred_element_type=F32)
                s = jnp.where(mask, s, -jnp.inf)
                sink = sink_ref[2 * pair + which]
                m = jnp.maximum(jnp.max(s, axis=-1, keepdims=True), sink)
                p = jnp.exp(s - m)
                denom = jnp.sum(p, axis=-1, keepdims=True) + jnp.exp(sink - m)
                inv.append(1.0 / denom)
                acc = acc + jnp.dot(p.astype(BF16), vpad[keys], preferred_element_type=F32)
            attn = acc * jnp.where(lo_lane, inv[0], inv[1])
            gate = g_ref[rows, cols].astype(F32)
            o_ref[rows, cols] = (attn * (gate / (1.0 + jnp.exp(-gate)))).astype(o_ref.dtype)


def _attention(proj3, pos3, sinks, inv_freq_row, *, tq=256):
    b, t, _ = proj3.shape
    assert t % tq == 0 and tq % WINDOW == 0
    sub_per_blk = tq // WINDOW
    cur = lambda name, w: pl.BlockSpec((None, tq, w), lambda bi, i, c=_col_block(name, w): (bi, i, c))
    prev = lambda name, w: pl.BlockSpec(
        (None, WINDOW, w),
        lambda bi, i, c=_col_block(name, w): (bi, jnp.maximum(i * sub_per_blk - 1, 0), c))
    return pl.pallas_call(
        functools.partial(_attn_kernel, tq=tq),
        grid=(b, t // tq),
        in_specs=[pl.BlockSpec(memory_space=pltpu.SMEM),
                  pl.BlockSpec((1, LANES), lambda bi, i: (0, 0)),
                  pl.BlockSpec((None, tq, 1), lambda bi, i: (bi, i, 0)),
                  pl.BlockSpec((None, WINDOW, 1), lambda bi, i: (bi, jnp.maximum(i * sub_per_blk - 1, 0), 0)),
                  cur("aq", ATTN_WIDTH), cur("ag", ATTN_WIDTH), cur("ak", KV_WIDTH), cur("av", KV_WIDTH),
                  prev("ak", KV_WIDTH), prev("av", KV_WIDTH)],
        out_specs=pl.BlockSpec((None, tq, ATTN_WIDTH), lambda bi, i: (bi, i, 0)),
        out_shape=jax.ShapeDtypeStruct((b, t, ATTN_WIDTH), BF16),
        compiler_params=pltpu.CompilerParams(
            dimension_semantics=("arbitrary", "arbitrary"), vmem_limit_bytes=VMEM_LIMIT),
        name="swa_attention",
    )(sinks, inv_freq_row, pos3, pos3, proj3, proj3, proj3, proj3, proj3, proj3)


def _anchor_rows(b, m):
    c, w = b.shape
    half = m // 2
    if m >= 8:
        g = c // m
        b3 = b.reshape(g, m, w)
        return jnp.broadcast_to(b3[:, half - 1:half, :], (g, m, w)).reshape(c, w)
    u = lax.broadcasted_iota(jnp.int32, (c, 1), 0) % m
    out = b
    for uu in range(m):
        shift = uu - (half - 1)
        if shift == 0:
            continue
        out = jnp.where(u == uu, pltpu.roll(b, shift % c, axis=0), out)
    return out


def _hgrn_kernel(lbp_ref, gain_ref, tril_ref, hq_ref, hf_ref, hi_ref, hg_ref, o_ref, st_ref, *, layer, chunk):
    c = chunk
    dk, dv = HGRN_KEY_DIM, HGRN_VALUE_DIM

    @pl.when(pl.program_id(1) == 0)
    def _():
        st_ref[...] = jnp.zeros_like(st_ref)

    lbp = lbp_ref[...].astype(F32)
    e = jnp.exp(lbp - jnp.max(lbp, axis=0, keepdims=True))
    lb = jnp.sum(e[:layer + 1], axis=0, keepdims=True) / jnp.sum(e, axis=0, keepdims=True)

    hf = hf_ref[...].astype(F32)
    ez = jnp.exp(-jnp.abs(hf))
    rz = 1.0 / (1.0 + ez)
    pos = hf >= 0
    sig = jnp.where(pos, rz, ez * rz)
    nsig = jnp.where(pos, ez * rz, rz)
    log_f = jnp.log(lb + (1.0 - lb) * sig)
    k = (1.0 - lb) * nsig
    hq = hq_ref[...].astype(F32)
    q = (hq / (1.0 + jnp.exp(-hq))) * (dk ** -0.5)
    v = hi_ref[...].astype(BF16)

    b = jnp.dot(tril_ref[...], log_f, preferred_element_type=F32, precision=lax.Precision.HIGHEST)
    b_last = b[c - 1:c, :]

    t_col = lax.broadcasted_iota(jnp.int32, (c, 1), 0)
    ti = lax.broadcasted_iota(jnp.int32, (c, c), 0)
    si = lax.broadcasted_iota(jnp.int32, (c, c), 1)
    txs = ti ^ si
    nt = (((1,), (1,)), ((), ()))
    tn = (((0,), (0,)), ((), ()))

    a_mats = [None] * HGRN_HEADS
    q16, k16 = q.astype(BF16), k.astype(BF16)
    for h in range(HGRN_HEADS):
        cols = slice(h * dk, (h + 1) * dk)
        gm = lax.dot_general(q16[:, cols], k16[:, cols], nt, preferred_element_type=F32)
        a_mats[h] = jnp.where(txs == 0, gm, 0.0)
    m = c
    while m >= 2:
        half = m // 2
        upper = (t_col % m) >= half
        z = jnp.exp(-jnp.abs(b - _anchor_rows(b, m)))
        qz = jnp.where(upper, q * z, 0.0).astype(BF16)
        kz = jnp.where(upper, 0.0, k * z).astype(BF16)
        sel = (txs // half) == 1
        for h in range(HGRN_HEADS):
            cols = slice(h * dk, (h + 1) * dk)
            gm = lax.dot_general(qz[:, cols], kz[:, cols], nt, preferred_element_type=F32)
            a_mats[h] = jnp.where(sel, gm, a_mats[h])
        m = half

    q_dec = (q * jnp.exp(b)).astype(BF16)
    k_dec = (k * jnp.exp(b_last - b)).astype(BF16)
    decay_last = jnp.exp(b_last)
    gain = gain_ref[...].astype(F32)
    hg = hg_ref[...].astype(F32)
    gate = hg / (1.0 + jnp.exp(-hg))
    for h in range(HGRN_HEADS):
        kc = slice(h * dk, (h + 1) * dk)
        vc = slice(h * dv, (h + 1) * dv)
        st = st_ref[h]
        o = jnp.dot(a_mats[h].astype(BF16), v[:, vc], preferred_element_type=F32)
        o = o + lax.dot_general(q_dec[:, kc], st.astype(BF16), nt, preferred_element_type=F32)
        st_ref[h] = st * decay_last[:, kc] + lax.dot_general(v[:, vc], k_dec[:, kc], tn,
                                                              preferred_element_type=F32)
        ms = jnp.mean(o * o, axis=-1, keepdims=True)
        y = (o * lax.rsqrt(ms + NORM_EPS)) * gain[:, vc]
        o_ref[:, vc] = (y * gate[:, vc]).astype(o_ref.dtype)


def _hgrn(proj3, lb_params, gain_row, layer, *, chunk=128):
    b, t, _ = proj3.shape
    assert t % chunk == 0 and chunk & (chunk - 1) == 0
    w = HGRN_WIDTH
    tril = jnp.asarray(np.tril(np.ones((chunk, chunk), np.float32)))
    cur = lambda name: pl.BlockSpec((None, chunk, w), lambda bi, i, c=_col_block(name, w): (bi, i, c))
    return pl.pallas_call(
        functools.partial(_hgrn_kernel, layer=layer, chunk=chunk),
        grid=(b, t // chunk),
        in_specs=[pl.BlockSpec(lb_params.shape, lambda bi, i: (0, 0)),
                  pl.BlockSpec((1, w), lambda bi, i: (0, 0)),
                  pl.BlockSpec((chunk, chunk), lambda bi, i: (0, 0)),
                  cur("hq"), cur("hf"), cur("hi"), cur("hg")],
        out_specs=pl.BlockSpec((None, chunk, w), lambda bi, i: (bi, i, 0)),
        out_shape=jax.ShapeDtypeStruct((b, t, w), BF16),
        scratch_shapes=[pltpu.VMEM((HGRN_HEADS, HGRN_VALUE_DIM, HGRN_KEY_DIM), F32)],
        compiler_params=pltpu.CompilerParams(
            dimension_semantics=("arbitrary", "arbitrary"), vmem_limit_bytes=VMEM_LIMIT),
        name="hgrn2",
    )(lb_params, gain_row, tril, proj3, proj3, proj3, proj3)


def _sigmoid(x):
    return 1.0 / (1.0 + jnp.exp(-x))


def _merge_kernel(ga_ref, gh_ref, ma_ref, mh_ref, x_ref, wa_ref, wh_ref, wo_ref, fg_ref, o_ref, *, final_norm):
    ya = jnp.dot(ga_ref[...], wa_ref[...], preferred_element_type=F32)
    yh = jnp.dot(gh_ref[...], wh_ref[...], preferred_element_type=F32)
    merged = _sigmoid(ma_ref[...].astype(F32)) * ya + _sigmoid(mh_ref[...].astype(F32)) * yh
    y = x_ref[...] + jnp.dot(merged.astype(BF16), wo_ref[...], preferred_element_type=F32)
    if final_norm:
        ms = jnp.mean(y * y, axis=-1, keepdims=True)
        y = (y * lax.rsqrt(ms + NORM_EPS)) * fg_ref[...]
    o_ref[...] = y


def _merge(ga, gh, proj2, x2d, wa, wh, wo, fgain, final_norm, *, tm=256):
    n_tok, d = x2d.shape
    assert n_tok % tm == 0
    const = lambda shape: pl.BlockSpec(shape, lambda i: (0, 0))
    return pl.pallas_call(
        functools.partial(_merge_kernel, final_norm=final_norm),
        grid=(n_tok // tm,),
        in_specs=[pl.BlockSpec((tm, ATTN_WIDTH), lambda i: (i, 0)),
                  pl.BlockSpec((tm, HGRN_WIDTH), lambda i: (i, 0)),
                  pl.BlockSpec((tm, d), lambda i, c=_col_block("ma", d): (i, c)),
                  pl.BlockSpec((tm, d), lambda i, c=_col_block("mh", d): (i, c)),
                  pl.BlockSpec((tm, d), lambda i: (i, 0)),
                  const(wa.shape), const(wh.shape), const(wo.shape), const((1, d))],
        out_specs=pl.BlockSpec((tm, d), lambda i: (i, 0)),
        out_shape=jax.ShapeDtypeStruct((n_tok, d), F32),
        compiler_params=pltpu.CompilerParams(
            dimension_semantics=("arbitrary",), vmem_limit_bytes=VMEM_LIMIT),
        name="merge_out",
    )(ga, gh, proj2, proj2, x2d, wa, wh, wo, fgain.reshape(1, d))


def kernel(x, positions, norm_gain, w_in, attn_sinks, hgrn_lower_bounds, hgrn_norm_gain, w_attn_out,
           w_hgrn_out, w_o, final_norm_gain):
    b, t, d = x.shape
    depth = w_in.shape[0]
    assert w_in.shape[2] == IN_WIDTH and _NEW_OFF["ma"][1] == d
    half = ATTN_HEAD_DIM // 2
    inv_freq = ROPE_THETA ** (-jnp.arange(half, dtype=F32) / half)
    inv_freq_row = jnp.tile(inv_freq, LANES // half).reshape(1, LANES)
    pos3 = positions.astype(jnp.int32).reshape(b, t, 1)

    x2d = x.reshape(b * t, d)
    for l in range(depth):
        w_l = jnp.concatenate(
            [w_in[l][:, _ORIG_OFF[n][0]:_ORIG_OFF[n][0] + _ORIG_OFF[n][1]] for n in _PROJ_ORDER],
            axis=1).astype(BF16)
        proj = _in_projection(x2d, norm_gain[l], w_l)
        proj3 = proj.reshape(b, t, IN_WIDTH)
        ga = _attention(proj3, pos3, attn_sinks[l].astype(F32), inv_freq_row)
        gh = _hgrn(proj3, hgrn_lower_bounds.astype(F32), hgrn_norm_gain[l].reshape(1, HGRN_WIDTH).astype(F32), l)
        x2d = _merge(ga.reshape(b * t, ATTN_WIDTH), gh.reshape(b * t, HGRN_WIDTH), proj, x2d,
                     w_attn_out[l].astype(BF16), w_hgrn_out[l].astype(BF16), w_o[l].astype(BF16),
                     final_norm_gain, l == depth - 1)
    return x2d.reshape(b, t, d)
```

```python
import functools

import numpy as np
import jax
import jax.numpy as jnp
from jax import lax
from jax.experimental import pallas as pl
from jax.experimental.pallas import tpu as pltpu

F32 = jnp.float32
BF16 = jnp.bfloat16

ATTN_HEADS = 16
ATTN_KV_HEADS = 4
ATTN_HEAD_DIM = 64
ATTN_GROUP = ATTN_HEADS // ATTN_KV_HEADS
ATTN_WIDTH = ATTN_HEADS * ATTN_HEAD_DIM
KV_WIDTH = ATTN_KV_HEADS * ATTN_HEAD_DIM
WINDOW = 128
ROPE_THETA = 10000.0
HGRN_HEADS = 8
HGRN_KEY_DIM = 128
HGRN_VALUE_DIM = 128
HGRN_WIDTH = HGRN_HEADS * HGRN_VALUE_DIM
NORM_EPS = 1e-6
LOG2_E = 1.4426950408889634

LANES = 128
VMEM_LIMIT = 56 * 1024 * 1024

_ORIG_SEGMENTS = (("aq", 1024), ("ak", 256), ("av", 256), ("ag", 1024), ("hq", 1024),
                  ("hf", 1024), ("hi", 1024), ("hg", 1024), ("ma", 2048), ("mh", 2048))
_PROJ_ORDER = ("aq", "ag", "hq", "hf", "hi", "hg", "ma", "mh", "ak", "av")


def _segment_table():
    orig, off = {}, 0
    for name, w in _ORIG_SEGMENTS:
        orig[name] = (off, w)
        off += w
    new, off = {}, 0
    for name in _PROJ_ORDER:
        new[name] = (off, orig[name][1])
        off += orig[name][1]
    return orig, new, off


_ORIG_OFF, _NEW_OFF, IN_WIDTH = _segment_table()


def _col_block(name, width):
    off = _NEW_OFF[name][0]
    assert off % width == 0
    return off // width


def _inproj_kernel(x_ref, g_ref, w_ref, o_ref, h_ref, *, row_chunk):
    @pl.when(pl.program_id(1) == 0)
    def _():
        def body(r, carry):
            rows = pl.ds(pl.multiple_of(r * row_chunk, row_chunk), row_chunk)
            x = x_ref[rows, :]
            ms = jnp.mean(x * x, axis=-1, keepdims=True)
            h_ref[rows, :] = ((x * lax.rsqrt(ms + NORM_EPS)) * g_ref[...]).astype(BF16)
            return carry
        lax.fori_loop(0, x_ref.shape[0] // row_chunk, body, 0)

    o_ref[...] = jnp.dot(h_ref[...], w_ref[...], preferred_element_type=F32).astype(o_ref.dtype)


def _in_projection(x2d, gain, w_bf16, *, tm=1024, tn=1536):
    n_tok, d = x2d.shape
    n_out = w_bf16.shape[1]
    assert n_tok % tm == 0 and n_out % tn == 0
    return pl.pallas_call(
        functools.partial(_inproj_kernel, row_chunk=128),
        grid=(n_tok // tm, n_out // tn),
        in_specs=[pl.BlockSpec((tm, d), lambda i, j: (i, 0)),
                  pl.BlockSpec((1, d), lambda i, j: (0, 0)),
                  pl.BlockSpec((d, tn), lambda i, j: (0, j))],
        out_specs=pl.BlockSpec((tm, tn), lambda i, j: (i, j)),
        out_shape=jax.ShapeDtypeStruct((n_tok, n_out), BF16),
        scratch_shapes=[pltpu.VMEM((tm, d), BF16)],
        compiler_params=pltpu.CompilerParams(
            dimension_semantics=("arbitrary", "arbitrary"), vmem_limit_bytes=VMEM_LIMIT),
        name="in_projection",
    )(x2d, gain.reshape(1, d), w_bf16)


def _rope_tables(pos_col, inv_freq_row):
    ang = pos_col.astype(F32) * inv_freq_row
    cos, sin = jnp.cos(ang), jnp.sin(ang)
    lane = lax.broadcasted_iota(jnp.int32, ang.shape, 1)
    first = (lane % ATTN_HEAD_DIM) < (ATTN_HEAD_DIM // 2)
    return cos, jnp.where(first, -sin, 0.0), jnp.where(first, 0.0, sin)


def _rope(x, cos, sin_first, sin_second):
    half = ATTN_HEAD_DIM // 2
    from_right = pltpu.roll(x, LANES - half, axis=1)
    from_left = pltpu.roll(x, half, axis=1)
    return x * cos + from_right * sin_first + from_left * sin_second


def _attn_kernel(sink_ref, invf_ref, pos_ref, q_ref, g_ref, k_ref, v_ref, o_ref, kpad_ref, vpad_ref, *, tq):
    blk = WINDOW
    nsub = tq // blk
    first_block = pl.program_id(1) == 0

    @pl.when(first_block)
    def _():
        kpad_ref[:, :blk, :] = jnp.zeros((2 * ATTN_KV_HEADS, blk, LANES), BF16)
        vpad_ref[:, :blk, :] = jnp.zeros((2 * ATTN_KV_HEADS, blk, LANES), BF16)

    @pl.when(jnp.logical_not(first_block))
    def _():
        kpad_ref[:, :blk, :] = kpad_ref[:, tq:, :]
        vpad_ref[:, :blk, :] = vpad_ref[:, tq:, :]

    cos, sin_first, sin_second = _rope_tables(pos_ref[...], invf_ref[...])
    lane = lax.broadcasted_iota(jnp.int32, (1, LANES), 1)
    lo_lane = lane < ATTN_HEAD_DIM

    for s in range(KV_WIDTH // LANES):
        cols = slice(s * LANES, (s + 1) * LANES)
        kk = _rope(k_ref[:, cols].astype(F32), cos, sin_first, sin_second)
        vv = v_ref[:, cols].astype(F32)
        kk_sw = pltpu.roll(kk, ATTN_HEAD_DIM, axis=1)
        vv_sw = pltpu.roll(vv, ATTN_HEAD_DIM, axis=1)
        for ref, own, swapped in ((kpad_ref, kk, kk_sw), (vpad_ref, vv, vv_sw)):
            ref[4 * s + 0, blk:, :] = jnp.where(lo_lane, own, 0.0).astype(BF16)
            ref[4 * s + 1, blk:, :] = jnp.where(lo_lane, 0.0, swapped).astype(BF16)
            ref[4 * s + 2, blk:, :] = jnp.where(lo_lane, swapped, 0.0).astype(BF16)
            ref[4 * s + 3, blk:, :] = jnp.where(lo_lane, 0.0, own).astype(BF16)

    r_idx = lax.broadcasted_iota(jnp.int32, (blk, 2 * blk), 0)
    c_idx = lax.broadcasted_iota(jnp.int32, (blk, 2 * blk), 1)
    band = (c_idx > r_idx) & (c_idx <= r_idx + blk)
    scale = (ATTN_HEAD_DIM ** -0.5) * LOG2_E
    nt = (((1,), (1,)), ((), ()))

    for sub in range(nsub):
        rows = slice(sub * blk, (sub + 1) * blk)
        keys = slice(sub * blk, sub * blk + 2 * blk)
        if sub == 0:
            mask = band & (c_idx >= jnp.where(first_block, blk, 0))
        else:
            mask = band
        for pair in range(ATTN_HEADS // 2):
            cols = slice(pair * LANES, (pair + 1) * LANES)
            g = (2 * pair) // ATTN_GROUP
            q2 = _rope(q_ref[rows, cols].astype(F32), cos[rows], sin_first[rows], sin_second[rows])
            q2 = (q2 * scale).astype(BF16)
            acc = jnp.zeros((blk, LANES), F32)
            inv = []
            for which in range(2):
                s = lax.dot_general(q2, kpad_ref[2 * g + which, keys, :], nt, preferred_element_type=F32)
                s = jnp.where(mask, s, -jnp.inf)
                sink = sink_ref[2 * pair + which] * LOG2_E
                m = jnp.maximum(jnp.max(s, axis=-1, keepdims=True), sink)
                p = jnp.exp2(s - m)
                denom = jnp.sum(p, axis=-1, keepdims=True) + jnp.exp2(sink - m)
                inv.append(1.0 / denom)
                acc = acc + jnp.dot(p.astype(BF16), vpad_ref[2 * g + which, keys, :], preferred_element_type=F32)
            attn = acc * jnp.where(lo_lane, inv[0], inv[1])
            gate = g_ref[rows, cols].astype(F32)
            o_ref[rows, cols] = (attn * (gate / (1.0 + jnp.exp(-gate)))).astype(o_ref.dtype)


def _attention(proj3, pos3, sinks, inv_freq_row, *, tq=512):
    b, t, _ = proj3.shape
    assert t % tq == 0 and tq % WINDOW == 0
    cur = lambda name, w: pl.BlockSpec((None, tq, w), lambda bi, i, c=_col_block(name, w): (bi, i, c))
    return pl.pallas_call(
        functools.partial(_attn_kernel, tq=tq),
        grid=(b, t // tq),
        in_specs=[pl.BlockSpec(memory_space=pltpu.SMEM),
                  pl.BlockSpec((1, LANES), lambda bi, i: (0, 0)),
                  pl.BlockSpec((None, tq, 1), lambda bi, i: (bi, i, 0)),
                  cur("aq", ATTN_WIDTH), cur("ag", ATTN_WIDTH), cur("ak", KV_WIDTH), cur("av", KV_WIDTH)],
        out_specs=pl.BlockSpec((None, tq, ATTN_WIDTH), lambda bi, i: (bi, i, 0)),
        out_shape=jax.ShapeDtypeStruct((b, t, ATTN_WIDTH), BF16),
        scratch_shapes=[pltpu.VMEM((2 * ATTN_KV_HEADS, WINDOW + tq, LANES), BF16),
                        pltpu.VMEM((2 * ATTN_KV_HEADS, WINDOW + tq, LANES), BF16)],
        compiler_params=pltpu.CompilerParams(
            dimension_semantics=("arbitrary", "arbitrary"), vmem_limit_bytes=VMEM_LIMIT),
        name="swa_attention",
    )(sinks, inv_freq_row, pos3, proj3, proj3, proj3, proj3)


def _anchor_rows(b, m):
    c, w = b.shape
    half = m // 2
    if m >= 8:
        g = c // m
        b3 = b.reshape(g, m, w)
        return jnp.broadcast_to(b3[:, half - 1:half, :], (g, m, w)).reshape(c, w)
    u = lax.broadcasted_iota(jnp.int32, (c, 1), 0) % m
    out = b
    for uu in range(m):
        shift = uu - (half - 1)
        if shift == 0:
            continue
        out = jnp.where(u == uu, pltpu.roll(b, shift % c, axis=0), out)
    return out


def _hgrn_kernel(lbp_ref, gain_ref, tril_ref, hq_ref, hf_ref, hi_ref, hg_ref, o_ref, st_ref, *, layer, chunk):
    c = chunk
    dk, dv = HGRN_KEY_DIM, HGRN_VALUE_DIM
    w = HGRN_HEADS * dk

    @pl.when(pl.program_id(1) == 0)
    def _():
        st_ref[...] = jnp.zeros_like(st_ref)

    lbp = lbp_ref[...].astype(F32)
    e = jnp.exp(lbp - jnp.max(lbp, axis=0, keepdims=True))
    lb = jnp.sum(e[:layer + 1], axis=0, keepdims=True) / jnp.sum(e, axis=0, keepdims=True)

    hf = hf_ref[...].astype(F32)
    ez = jnp.exp(-jnp.abs(hf))
    rz = 1.0 / (1.0 + ez)
    pos = hf >= 0
    sig = jnp.where(pos, rz, ez * rz)
    nsig = jnp.where(pos, ez * rz, rz)
    log2_f = jnp.log(lb + (1.0 - lb) * sig) * LOG2_E
    k16 = ((1.0 - lb) * nsig).astype(BF16)
    hq = hq_ref[...].astype(F32)
    q16 = ((hq / (1.0 + jnp.exp(-hq))) * (dk ** -0.5)).astype(BF16)
    v = hi_ref[...].astype(BF16)

    f_hi = log2_f.astype(BF16)
    f_lo = (log2_f - f_hi.astype(F32)).astype(BF16)
    cs = jnp.dot(tril_ref[...], jnp.concatenate([f_hi, f_lo], axis=1), preferred_element_type=F32)
    b = cs[:, :w] + cs[:, w:]
    b_last = b[c - 1:c, :]

    ti = lax.broadcasted_iota(jnp.int32, (c, c), 0)
    si = lax.broadcasted_iota(jnp.int32, (c, c), 1)
    txs = ti ^ si
    causal = ti > si
    nt = (((1,), (1,)), ((), ()))
    tn = (((0,), (0,)), ((), ()))

    a_mats = [None] * HGRN_HEADS
    for h in range(HGRN_HEADS):
        cols = slice(h * dk, (h + 1) * dk)
        gm = lax.dot_general(q16[:, cols], k16[:, cols], nt, preferred_element_type=F32)
        a_mats[h] = jnp.where(txs == 0, gm, 0.0)
    m = c
    while m >= 2:
        half = m // 2
        z = jnp.exp2(-jnp.abs(b - _anchor_rows(b, m))).astype(BF16)
        qz = q16 * z
        kz = k16 * z
        sel = ((txs // half) == 1) & causal
        for h in range(HGRN_HEADS):
            cols = slice(h * dk, (h + 1) * dk)
            gm = lax.dot_general(qz[:, cols], kz[:, cols], nt, preferred_element_type=F32)
            a_mats[h] = jnp.where(sel, gm, a_mats[h])
        m = half

    q_dec = q16 * jnp.exp2(b).astype(BF16)
    k_dec = k16 * jnp.exp2(b_last - b).astype(BF16)
    decay_last = jnp.exp2(b_last)
    gain = gain_ref[...].astype(F32)
    hg = hg_ref[...].astype(F32)
    gate = hg / (1.0 + jnp.exp(-hg))
    for h in range(HGRN_HEADS):
        kc = slice(h * dk, (h + 1) * dk)
        vc = slice(h * dv, (h + 1) * dv)
        st = st_ref[h]
        o = jnp.dot(a_mats[h].astype(BF16), v[:, vc], preferred_element_type=F32)
        o = o + lax.dot_general(q_dec[:, kc], st.astype(BF16), nt, preferred_element_type=F32)
        st_ref[h] = st * decay_last[:, kc] + lax.dot_general(v[:, vc], k_dec[:, kc], tn,
                                                              preferred_element_type=F32)
        ms = jnp.mean(o * o, axis=-1, keepdims=True)
        y = (o * lax.rsqrt(ms + NORM_EPS)) * gain[:, vc]
        o_ref[:, vc] = (y * gate[:, vc]).astype(o_ref.dtype)


def _hgrn(proj3, lb_params, gain_row, layer, *, chunk=128):
    b, t, _ = proj3.shape
    assert t % chunk == 0 and chunk & (chunk - 1) == 0
    w = HGRN_WIDTH
    tril = jnp.asarray(np.tril(np.ones((chunk, chunk), np.float32)), dtype=BF16)
    cur = lambda name: pl.BlockSpec((None, chunk, w), lambda bi, i, c=_col_block(name, w): (bi, i, c))
    return pl.pallas_call(
        functools.partial(_hgrn_kernel, layer=layer, chunk=chunk),
        grid=(b, t // chunk),
        in_specs=[pl.BlockSpec(lb_params.shape, lambda bi, i: (0, 0)),
                  pl.BlockSpec((1, w), lambda bi, i: (0, 0)),
                  pl.BlockSpec((chunk, chunk), lambda bi, i: (0, 0)),
                  cur("hq"), cur("hf"), cur("hi"), cur("hg")],
        out_specs=pl.BlockSpec((None, chunk, w), lambda bi, i: (bi, i, 0)),
        out_shape=jax.ShapeDtypeStruct((b, t, w), BF16),
        scratch_shapes=[pltpu.VMEM((HGRN_HEADS, HGRN_VALUE_DIM, HGRN_KEY_DIM), F32)],
        compiler_params=pltpu.CompilerParams(
            dimension_semantics=("arbitrary", "arbitrary"), vmem_limit_bytes=VMEM_LIMIT),
        name="hgrn2",
    )(lb_params, gain_row, tril, proj3, proj3, proj3, proj3)


def _sigmoid(x):
    return 1.0 / (1.0 + jnp.exp(-x))


def _merge_kernel(ga_ref, gh_ref, ma_ref, mh_ref, x_ref, wa_ref, wh_ref, wo_ref, fg_ref, o_ref, *, final_norm):
    ya = jnp.dot(ga_ref[...], wa_ref[...], preferred_element_type=F32)
    yh = jnp.dot(gh_ref[...], wh_ref[...], preferred_element_type=F32)
    merged = _sigmoid(ma_ref[...].astype(F32)) * ya + _sigmoid(mh_ref[...].astype(F32)) * yh
    y = x_ref[...] + jnp.dot(merged.astype(BF16), wo_ref[...], preferred_element_type=F32)
    if final_norm:
        ms = jnp.mean(y * y, axis=-1, keepdims=True)
        y = (y * lax.rsqrt(ms + NORM_EPS)) * fg_ref[...]
    o_ref[...] = y


def _merge(ga, gh, proj2, x2d, wa, wh, wo, fgain, final_norm, *, tm=256):
    n_tok, d = x2d.shape
    assert n_tok % tm == 0
    const = lambda shape: pl.BlockSpec(shape, lambda i: (0, 0))
    return pl.pallas_call(
        functools.partial(_merge_kernel, final_norm=final_norm),
        grid=(n_tok // tm,),
        in_specs=[pl.BlockSpec((tm, ATTN_WIDTH), lambda i: (i, 0)),
                  pl.BlockSpec((tm, HGRN_WIDTH), lambda i: (i, 0)),
                  pl.BlockSpec((tm, d), lambda i, c=_col_block("ma", d): (i, c)),
                  pl.BlockSpec((tm, d), lambda i, c=_col_block("mh", d): (i, c)),
                  pl.BlockSpec((tm, d), lambda i: (i, 0)),
                  const(wa.shape), const(wh.shape), const(wo.shape), const((1, d))],
        out_specs=pl.BlockSpec((tm, d), lambda i: (i, 0)),
        out_shape=jax.ShapeDtypeStruct((n_tok, d), F32),
        compiler_params=pltpu.CompilerParams(
            dimension_semantics=("arbitrary",), vmem_limit_bytes=VMEM_LIMIT),
        name="merge_out",
    )(ga, gh, proj2, proj2, x2d, wa, wh, wo, fgain.reshape(1, d))


def kernel(x, positions, norm_gain, w_in, attn_sinks, hgrn_lower_bounds, hgrn_norm_gain, w_attn_out,
           w_hgrn_out, w_o, final_norm_gain):
    b, t, d = x.shape
    depth = w_in.shape[0]
    assert w_in.shape[2] == IN_WIDTH and _NEW_OFF["ma"][1] == d
    half = ATTN_HEAD_DIM // 2
    inv_freq = ROPE_THETA ** (-jnp.arange(half, dtype=F32) / half)
    inv_freq_row = jnp.tile(inv_freq, LANES // half).reshape(1, LANES)
    pos3 = positions.astype(jnp.int32).reshape(b, t, 1)

    x2d = x.reshape(b * t, d)
    for l in range(depth):
        w_l = jnp.concatenate(
            [w_in[l][:, _ORIG_OFF[n][0]:_ORIG_OFF[n][0] + _ORIG_OFF[n][1]] for n in _PROJ_ORDER],
            axis=1).astype(BF16)
        proj = _in_projection(x2d, norm_gain[l], w_l)
        proj3 = proj.reshape(b, t, IN_WIDTH)
        ga = _attention(proj3, pos3, attn_sinks[l].astype(F32), inv_freq_row)
        gh = _hgrn(proj3, hgrn_lower_bounds.astype(F32), hgrn_norm_gain[l].reshape(1, HGRN_WIDTH).astype(F32), l)
        x2d = _merge(ga.reshape(b * t, ATTN_WIDTH), gh.reshape(b * t, HGRN_WIDTH), proj, x2d,
                     w_attn_out[l].astype(BF16), w_hgrn_out[l].astype(BF16), w_o[l].astype(BF16),
                     final_norm_gain, l == depth - 1)
    return x2d.reshape(b, t, d)
```

```python
import functools

import numpy as np
import jax
import jax.numpy as jnp
from jax import lax
from jax.experimental import pallas as pl
from jax.experimental.pallas import tpu as pltpu

F32 = jnp.float32
BF16 = jnp.bfloat16

ATTN_HEADS = 16
ATTN_KV_HEADS = 4
ATTN_HEAD_DIM = 64
ATTN_GROUP = ATTN_HEADS // ATTN_KV_HEADS
ATTN_WIDTH = ATTN_HEADS * ATTN_HEAD_DIM
KV_WIDTH = ATTN_KV_HEADS * ATTN_HEAD_DIM
WINDOW = 128
ROPE_THETA = 10000.0
HGRN_HEADS = 8
HGRN_KEY_DIM = 128
HGRN_VALUE_DIM = 128
HGRN_WIDTH = HGRN_HEADS * HGRN_VALUE_DIM
HGRN_CHUNK = 128
NORM_EPS = 1e-6
LOG2_E = 1.4426950408889634

LANES = 128
MXU_COLS = 256
VMEM_LIMIT = 56 * 1024 * 1024

_ORIG_SEGMENTS = (("aq", 1024), ("ak", 256), ("av", 256), ("ag", 1024), ("hq", 1024),
                  ("hf", 1024), ("hi", 1024), ("hg", 1024), ("ma", 2048), ("mh", 2048))
_PROJ_ORDER = ("aq", "ag", "hq", "hf", "hi", "hg", "ak", "av", "ma", "mh")
_MIXER_WIDTH = 6656


def _segment_table():
    orig, off = {}, 0
    for name, w in _ORIG_SEGMENTS:
        orig[name] = (off, w)
        off += w
    new, off = {}, 0
    for name in _PROJ_ORDER:
        new[name] = (off, orig[name][1])
        off += orig[name][1]
    return orig, new, off


_ORIG_OFF, _NEW_OFF, IN_WIDTH = _segment_table()


def _cols(name):
    off, width = _NEW_OFF[name]
    return slice(off, off + width)


def _rope_tables(pos_col, inv_freq_row):
    ang = pos_col.astype(F32) * inv_freq_row
    cos, sin = jnp.cos(ang), jnp.sin(ang)
    lane = lax.broadcasted_iota(jnp.int32, ang.shape, 1)
    first = (lane % ATTN_HEAD_DIM) < (ATTN_HEAD_DIM // 2)
    return cos, jnp.where(first, -sin, 0.0), jnp.where(first, 0.0, sin)


def _rope(x, cos, sin_first, sin_second):
    half = ATTN_HEAD_DIM // 2
    from_right = pltpu.roll(x, LANES - half, axis=1)
    from_left = pltpu.roll(x, half, axis=1)
    return x * cos + from_right * sin_first + from_left * sin_second


def _silu(x):
    return x / (1.0 + jnp.exp(-x))


def _attn_prepare(pos_ref, invf_ref, pj, kpad_ref, vpad_ref, trig_ref, *, tb):
    blk = WINDOW
    kpad_ref[:, :blk, :] = kpad_ref[:, tb:, :]
    vpad_ref[:, :blk, :] = vpad_ref[:, tb:, :]
    cos, sin_first, sin_second = _rope_tables(pos_ref[...], invf_ref[...])
    trig_ref[0] = cos
    trig_ref[1] = sin_first
    trig_ref[2] = sin_second
    lo_lane = lax.broadcasted_iota(jnp.int32, (1, LANES), 1) < ATTN_HEAD_DIM
    k_off, v_off = _NEW_OFF["ak"][0], _NEW_OFF["av"][0]
    for s in range(KV_WIDTH // LANES):
        kk = _rope(pj[:, k_off + s * LANES:k_off + (s + 1) * LANES].astype(F32), cos, sin_first, sin_second)
        vv = pj[:, v_off + s * LANES:v_off + (s + 1) * LANES].astype(F32)
        kk_sw = pltpu.roll(kk, ATTN_HEAD_DIM, axis=1)
        vv_sw = pltpu.roll(vv, ATTN_HEAD_DIM, axis=1)
        for ref, own, swapped in ((kpad_ref, kk, kk_sw), (vpad_ref, vv, vv_sw)):
            ref[4 * s + 0, blk:, :] = jnp.where(lo_lane, own, 0.0).astype(BF16)
            ref[4 * s + 1, blk:, :] = jnp.where(lo_lane, 0.0, swapped).astype(BF16)
            ref[4 * s + 2, blk:, :] = jnp.where(lo_lane, swapped, 0.0).astype(BF16)
            ref[4 * s + 3, blk:, :] = jnp.where(lo_lane, 0.0, own).astype(BF16)
    yield _PREP_COST


def _attn_unit(sub, pair, first_of_seq, sink_ref, pj, kpad_ref, vpad_ref, trig_ref, ga_ref):
    blk = WINDOW
    rows = slice(sub * blk, (sub + 1) * blk)
    keys = slice(sub * blk, sub * blk + 2 * blk)
    lo_lane = lax.broadcasted_iota(jnp.int32, (1, LANES), 1) < ATTN_HEAD_DIM
    r_idx = lax.broadcasted_iota(jnp.int32, (blk, 2 * blk), 0)
    c_idx = lax.broadcasted_iota(jnp.int32, (blk, 2 * blk), 1)
    mask = (c_idx > r_idx) & (c_idx <= r_idx + blk)
    if sub == 0:
        mask = mask & (c_idx >= jnp.where(first_of_seq, blk, 0))
    scale = (ATTN_HEAD_DIM ** -0.5) * LOG2_E
    nt = (((1,), (1,)), ((), ()))
    q_off, g_off = _NEW_OFF["aq"][0], _NEW_OFF["ag"][0]
    cols = slice(pair * LANES, (pair + 1) * LANES)
    g = (2 * pair) // ATTN_GROUP
    q2 = _rope(pj[rows, q_off + pair * LANES:q_off + (pair + 1) * LANES].astype(F32),
               trig_ref[0, rows, :], trig_ref[1, rows, :], trig_ref[2, rows, :])
    q2 = (q2 * scale).astype(BF16)
    acc = jnp.zeros((blk, LANES), F32)
    inv = []
    for which in range(2):
        s = lax.dot_general(q2, kpad_ref[2 * g + which, keys, :], nt, preferred_element_type=F32)
        s = jnp.where(mask, s, -jnp.inf)
        sink = sink_ref[2 * pair + which] * LOG2_E
        m = jnp.maximum(jnp.max(s, axis=-1, keepdims=True), sink)
        p = jnp.exp2(s - m)
        denom = jnp.sum(p, axis=-1, keepdims=True) + jnp.exp2(sink - m)
        inv.append(1.0 / denom)
        acc = acc + jnp.dot(p.astype(BF16), vpad_ref[2 * g + which, keys, :], preferred_element_type=F32)
    attn = acc * jnp.where(lo_lane, inv[0], inv[1])
    gate = pj[rows, g_off + pair * LANES:g_off + (pair + 1) * LANES].astype(F32)
    ga_ref[rows, cols] = (attn * _silu(gate)).astype(ga_ref.dtype)
    yield _UNIT_COST


def _anchor_rows(b, m):
    c, w = b.shape
    half = m // 2
    if m >= 8:
        g = c // m
        b3 = b.reshape(g, m, w)
        return jnp.broadcast_to(b3[:, half - 1:half, :], (g, m, w)).reshape(c, w)
    u = lax.broadcasted_iota(jnp.int32, (c, 1), 0) % m
    out = b
    for uu in range(m):
        shift = uu - (half - 1)
        if shift == 0:
            continue
        out = jnp.where(u == uu, pltpu.roll(b, shift % c, axis=0), out)
    return out


def _hgrn_chunk(ci, reset, lbp_ref, gain_ref, tril_ref, pj, gh_ref, st_ref, *, layer):
    c = HGRN_CHUNK
    dk, dv = HGRN_KEY_DIM, HGRN_VALUE_DIM
    w = HGRN_HEADS * dk
    rows = slice(ci * c, (ci + 1) * c)

    lbp = lbp_ref[...].astype(F32)
    e = jnp.exp(lbp - jnp.max(lbp, axis=0, keepdims=True))
    lb = jnp.sum(e[:layer + 1], axis=0, keepdims=True) / jnp.sum(e, axis=0, keepdims=True)

    hf = pj[rows, _cols("hf")].astype(F32)
    ez = jnp.exp(-jnp.abs(hf))
    rz = 1.0 / (1.0 + ez)
    pos = hf >= 0
    sig = jnp.where(pos, rz, ez * rz)
    nsig = jnp.where(pos, ez * rz, rz)
    log2_f = jnp.log(lb + (1.0 - lb) * sig) * LOG2_E
    k16 = ((1.0 - lb) * nsig).astype(BF16)
    q16 = (_silu(pj[rows, _cols("hq")].astype(F32)) * (dk ** -0.5)).astype(BF16)
    v = pj[rows, _cols("hi")]

    f_hi = log2_f.astype(BF16)
    f_lo = (log2_f - f_hi.astype(F32)).astype(BF16)
    cs = jnp.dot(tril_ref[...], jnp.concatenate([f_hi, f_lo], axis=1), preferred_element_type=F32)
    b = cs[:, :w] + cs[:, w:]
    b_last = b[c - 1:c, :]
    yield 0.25 * _CHUNK_COST

    ti = lax.broadcasted_iota(jnp.int32, (c, c), 0)
    si = lax.broadcasted_iota(jnp.int32, (c, c), 1)
    txs = ti ^ si
    causal = ti > si
    nt = (((1,), (1,)), ((), ()))
    tn = (((0,), (0,)), ((), ()))

    a_mats = [None] * HGRN_HEADS
    for h in range(HGRN_HEADS):
        cols = slice(h * dk, (h + 1) * dk)
        gm = lax.dot_general(q16[:, cols], k16[:, cols], nt, preferred_element_type=F32)
        a_mats[h] = jnp.where(txs == 0, gm, 0.0)
    m = c
    while m >= 2:
        half = m // 2
        z = jnp.exp2(-jnp.abs(b - _anchor_rows(b, m))).astype(BF16)
        qz = q16 * z
        kz = k16 * z
        sel = ((txs // half) == 1) & causal
        for h in range(HGRN_HEADS):
            cols = slice(h * dk, (h + 1) * dk)
            gm = lax.dot_general(qz[:, cols], kz[:, cols], nt, preferred_element_type=F32)
            a_mats[h] = jnp.where(sel, gm, a_mats[h])
        m = half
        yield 0.075 * _CHUNK_COST

    q_dec = q16 * jnp.exp2(b).astype(BF16)
    k_dec = k16 * jnp.exp2(b_last - b).astype(BF16)
    decay_last = jnp.exp2(b_last)
    gain = gain_ref[...].astype(F32)
    gate = _silu(pj[rows, _cols("hg")].astype(F32))
    for h in range(HGRN_HEADS):
        kc = slice(h * dk, (h + 1) * dk)
        vc = slice(h * dv, (h + 1) * dv)
        st = st_ref[h]
        if reset is not None:
            st = jnp.where(reset, 0.0, st)
        o = jnp.dot(a_mats[h].astype(BF16), v[:, vc], preferred_element_type=F32)
        o = o + lax.dot_general(q_dec[:, kc], st.astype(BF16), nt, preferred_element_type=F32)
        st_ref[h] = st * decay_last[:, kc] + lax.dot_general(v[:, vc], k_dec[:, kc], tn,
                                                              preferred_element_type=F32)
        ms = jnp.mean(o * o, axis=-1, keepdims=True)
        y = (o * lax.rsqrt(ms + NORM_EPS)) * gain[:, vc]
        gh_ref[rows, vc] = (y * gate[:, vc]).astype(gh_ref.dtype)
        if h % 2 == 1:
            yield 0.05 * _CHUNK_COST


_UNIT_COST, _CHUNK_COST, _PREP_COST, _NORM_COST = 1.0, 8.0, 3.5, 3.5


class _SlotView:
    def __init__(self, ref, slot):
        self._ref, self._slot = ref, slot

    def __getitem__(self, idx):
        rows, cols = idx
        return self._ref[self._slot, rows, cols]


def _mixer_schedule(n_steps, n_sub, n_chunks):
    units = [(sub, pair) for sub in range(n_sub) for pair in range(ATTN_HEADS // 2)]
    hgrn_steps = list(range(1, 1 + n_chunks))
    assert n_steps > n_chunks + 1
    unit_cost, chunk_cost, prep_cost = _UNIT_COST, _CHUNK_COST, _PREP_COST + _NORM_COST
    total = unit_cost * len(units) + chunk_cost * n_chunks + prep_cost
    target = total / n_steps
    plan, nxt = [], 0
    for k in range(n_steps):
        fixed = prep_cost if k == 0 else (chunk_cost if k in hgrn_steps else 0.0)
        n_units = max(0, int(round((target - fixed) / unit_cost)))
        if k == n_steps - 1:
            n_units = len(units) - nxt
        n_units = min(n_units, len(units) - nxt)
        plan.append({"prepare": k == 0, "chunk": (k - 1) if k in hgrn_steps else None,
                     "units": units[nxt:nxt + n_units]})
        nxt += n_units
    assert nxt == len(units)
    return plan


def _trunk_kernel(sink_ref, invf_ref, lbp_ref, hgain_ref, tril_ref, ngain_ref, pos_ref, x_ref, w_ref,
                  ga_ref, gh_ref, gates_ref,
                  h_ref, proj_ref, kpad_ref, vpad_ref, trig_ref, st_ref,
                  *, tb, tn, plan, blocks_per_seq, layer, row_chunk):
    s = pl.program_id(0)
    j = pl.program_id(1)
    write_slot = lax.rem(s, 2)
    read_slot = 1 - write_slot
    first_of_seq = lax.rem(s + blocks_per_seq - 1, blocks_per_seq) == 0

    @pl.when((s == 0) & (j == 0))
    def _():
        proj_ref[1] = jnp.zeros(proj_ref.shape[1:], proj_ref.dtype)
        kpad_ref[...] = jnp.zeros_like(kpad_ref)
        vpad_ref[...] = jnp.zeros_like(vpad_ref)
        st_ref[...] = jnp.zeros_like(st_ref)

    pj = _SlotView(proj_ref, read_slot)

    def mixer_stages(k):
        piece = plan[k]
        if piece["prepare"]:
            yield from _attn_prepare(pos_ref, invf_ref, pj, kpad_ref, vpad_ref, trig_ref, tb=tb)
        if piece["chunk"] is not None:
            ci = piece["chunk"]
            yield from _hgrn_chunk(ci, first_of_seq if ci == 0 else None, lbp_ref, hgain_ref, tril_ref, pj,
                                   gh_ref, st_ref, layer=layer)
        for sub, pair in piece["units"]:
            yield from _attn_unit(sub, pair, first_of_seq, sink_ref, pj, kpad_ref, vpad_ref, trig_ref, ga_ref)

    def project(k, n):
        res = jnp.dot(h_ref[...], w_ref[:, n * MXU_COLS:(n + 1) * MXU_COLS], preferred_element_type=F32)
        c0 = k * tn + n * MXU_COLS
        if c0 < _MIXER_WIDTH:
            proj_ref[write_slot, :, c0:c0 + MXU_COLS] = res.astype(BF16)
        else:
            gates_ref[:, c0 - _MIXER_WIDTH:c0 - _MIXER_WIDTH + MXU_COLS] = res.astype(BF16)

    def step(k):
        if k == 0:
            def body(r, carry):
                rows = pl.ds(pl.multiple_of(r * row_chunk, row_chunk), row_chunk)
                x = x_ref[rows, :]
                ms = jnp.mean(x * x, axis=-1, keepdims=True)
                h_ref[rows, :] = ((x * lax.rsqrt(ms + NORM_EPS)) * ngain_ref[...]).astype(BF16)
                return carry
            lax.fori_loop(0, tb // row_chunk, body, 0, unroll=True)
        n_proj = tn // MXU_COLS
        piece = plan[k]
        work = (_PREP_COST if piece["prepare"] else 0.0) + (_CHUNK_COST if piece["chunk"] is not None else 0.0) \
            + _UNIT_COST * len(piece["units"])
        done, credit = 0, 0.5
        for cost in mixer_stages(k):
            credit += cost * n_proj / work
            while credit >= 1.0 and done < n_proj:
                project(k, done)
                done, credit = done + 1, credit - 1.0
        for n in range(done, n_proj):
            project(k, n)

    for k in range(len(plan)):
        pl.when(j == k)(functools.partial(step, k))


def _trunk(x2d, pos2, ngain, w_bf16, sinks, inv_freq_row, lb_params, hgain_row, layer, *, seq_len, tb=512, tn=1536):
    n_tok, d = x2d.shape
    n_out = w_bf16.shape[1]
    assert n_tok % tb == 0 and seq_len % tb == 0 and n_out % tn == 0 and tn % MXU_COLS == 0 and _MIXER_WIDTH % MXU_COLS == 0 and tb % HGRN_CHUNK == 0 and tb % WINDOW == 0
    n_blocks = n_tok // tb
    n_steps = n_out // tn
    plan = _mixer_schedule(n_steps, tb // WINDOW, tb // HGRN_CHUNK)
    tril = jnp.asarray(np.tril(np.ones((HGRN_CHUNK, HGRN_CHUNK), np.float32)), dtype=BF16)
    gate_w = n_out - _MIXER_WIDTH
    last = n_blocks - 1
    proj_blk = lambda s, j: (jnp.minimum(s, last), 0)
    mix_blk = lambda s, j: (jnp.maximum(s - 1, 0), 0)
    const = lambda shape: pl.BlockSpec(shape, lambda s, j: (0, 0))
    return pl.pallas_call(
        functools.partial(_trunk_kernel, tb=tb, tn=tn, plan=plan, blocks_per_seq=seq_len // tb, layer=layer,
                          row_chunk=128),
        grid=(n_blocks + 1, n_steps),
        in_specs=[pl.BlockSpec(memory_space=pltpu.SMEM),
                  const((1, LANES)), const(lb_params.shape), const((1, HGRN_WIDTH)),
                  const((HGRN_CHUNK, HGRN_CHUNK)), const((1, d)),
                  pl.BlockSpec((tb, 1), mix_blk),
                  pl.BlockSpec((tb, d), proj_blk),
                  pl.BlockSpec((d, tn), lambda s, j: (0, j))],
        out_specs=[pl.BlockSpec((tb, ATTN_WIDTH), mix_blk),
                   pl.BlockSpec((tb, HGRN_WIDTH), mix_blk),
                   pl.BlockSpec((tb, gate_w), proj_blk)],
        out_shape=[jax.ShapeDtypeStruct((n_tok, ATTN_WIDTH), BF16),
                   jax.ShapeDtypeStruct((n_tok, HGRN_WIDTH), BF16),
                   jax.ShapeDtypeStruct((n_tok, gate_w), BF16)],
        scratch_shapes=[pltpu.VMEM((tb, d), BF16),
                        pltpu.VMEM((2, tb, _MIXER_WIDTH), BF16),
                        pltpu.VMEM((2 * ATTN_KV_HEADS, WINDOW + tb, LANES), BF16),
                        pltpu.VMEM((2 * ATTN_KV_HEADS, WINDOW + tb, LANES), BF16),
                        pltpu.VMEM((3, tb, LANES), F32),
                        pltpu.VMEM((HGRN_HEADS, HGRN_VALUE_DIM, HGRN_KEY_DIM), F32)],
        compiler_params=pltpu.CompilerParams(
            dimension_semantics=("arbitrary", "arbitrary"), vmem_limit_bytes=VMEM_LIMIT),
        name="trunk",
    )(sinks, inv_freq_row, lb_params, hgain_row, tril, ngain.reshape(1, d), pos2, x2d, w_bf16)


def _sigmoid(x):
    return 1.0 / (1.0 + jnp.exp(-x))


def _merge_kernel(ga_ref, gh_ref, ma_ref, mh_ref, x_ref, wa_ref, wh_ref, wo_ref, fg_ref, o_ref, *, final_norm):
    ya = jnp.dot(ga_ref[...], wa_ref[...], preferred_element_type=F32)
    yh = jnp.dot(gh_ref[...], wh_ref[...], preferred_element_type=F32)
    merged = _sigmoid(ma_ref[...].astype(F32)) * ya + _sigmoid(mh_ref[...].astype(F32)) * yh
    y = x_ref[...] + jnp.dot(merged.astype(BF16), wo_ref[...], preferred_element_type=F32)
    if final_norm:
        ms = jnp.mean(y * y, axis=-1, keepdims=True)
        y = (y * lax.rsqrt(ms + NORM_EPS)) * fg_ref[...]
    o_ref[...] = y


def _merge(ga, gh, gates, x2d, wa, wh, wo, fgain, final_norm, *, tm=256):
    n_tok, d = x2d.shape
    assert n_tok % tm == 0
    const = lambda shape: pl.BlockSpec(shape, lambda i: (0, 0))
    return pl.pallas_call(
        functools.partial(_merge_kernel, final_norm=final_norm),
        grid=(n_tok // tm,),
        in_specs=[pl.BlockSpec((tm, ATTN_WIDTH), lambda i: (i, 0)),
                  pl.BlockSpec((tm, HGRN_WIDTH), lambda i: (i, 0)),
                  pl.BlockSpec((tm, d), lambda i: (i, 0)),
                  pl.BlockSpec((tm, d), lambda i: (i, 1)),
                  pl.BlockSpec((tm, d), lambda i: (i, 0)),
                  const(wa.shape), const(wh.shape), const(wo.shape), const((1, d))],
        out_specs=pl.BlockSpec((tm, d), lambda i: (i, 0)),
        out_shape=jax.ShapeDtypeStruct((n_tok, d), F32),
        compiler_params=pltpu.CompilerParams(
            dimension_semantics=("arbitrary",), vmem_limit_bytes=VMEM_LIMIT),
        name="merge_out",
    )(ga, gh, gates, gates, x2d, wa, wh, wo, fgain.reshape(1, d))


def kernel(x, positions, norm_gain, w_in, attn_sinks, hgrn_lower_bounds, hgrn_norm_gain, w_attn_out,
           w_hgrn_out, w_o, final_norm_gain):
    b, t, d = x.shape
    depth = w_in.shape[0]
    assert w_in.shape[2] == IN_WIDTH and _NEW_OFF["ma"] == (_MIXER_WIDTH, d) and _NEW_OFF["mh"][1] == d
    half = ATTN_HEAD_DIM // 2
    inv_freq = ROPE_THETA ** (-jnp.arange(half, dtype=F32) / half)
    inv_freq_row = jnp.tile(inv_freq, LANES // half).reshape(1, LANES)
    pos2 = positions.astype(jnp.int32).reshape(b * t, 1)

    x2d = x.reshape(b * t, d)
    for l in range(depth):
        w_l = jnp.concatenate(
            [w_in[l][:, _ORIG_OFF[n][0]:_ORIG_OFF[n][0] + _ORIG_OFF[n][1]] for n in _PROJ_ORDER],
            axis=1).astype(BF16)
        ga, gh, gates = _trunk(x2d, pos2, norm_gain[l], w_l, attn_sinks[l].astype(F32), inv_freq_row,
                               hgrn_lower_bounds.astype(F32),
                               hgrn_norm_gain[l].reshape(1, HGRN_WIDTH).astype(F32), l, seq_len=t)
        x2d = _merge(ga, gh, gates, x2d,
                     w_attn_out[l].astype(BF16), w_hgrn_out[l].astype(BF16), w_o[l].astype(BF16),
                     final_norm_gain, l == depth - 1)
    return x2d.reshape(b, t, d)
```

```python
import functools

import numpy as np
import jax
import jax.numpy as jnp
from jax import lax
from jax.experimental import pallas as pl
from jax.experimental.pallas import tpu as pltpu

F32 = jnp.float32
BF16 = jnp.bfloat16

ATTN_HEADS = 16
ATTN_KV_HEADS = 4
ATTN_HEAD_DIM = 64
ATTN_GROUP = ATTN_HEADS // ATTN_KV_HEADS
ATTN_WIDTH = ATTN_HEADS * ATTN_HEAD_DIM
KV_WIDTH = ATTN_KV_HEADS * ATTN_HEAD_DIM
WINDOW = 128
ROPE_THETA = 10000.0
HGRN_HEADS = 8
HGRN_KEY_DIM = 128
HGRN_VALUE_DIM = 128
HGRN_WIDTH = HGRN_HEADS * HGRN_VALUE_DIM
NORM_EPS = 1e-6
LOG2_E = 1.4426950408889634

LANES = 128
VMEM_LIMIT = 56 * 1024 * 1024

_ORIG_SEGMENTS = (("aq", 1024), ("ak", 256), ("av", 256), ("ag", 1024), ("hq", 1024),
                  ("hf", 1024), ("hi", 1024), ("hg", 1024), ("ma", 2048), ("mh", 2048))
_PROJ_ORDER = ("aq", "ag", "hq", "hf", "hi", "hg", "ma", "mh", "ak", "av")


def _segment_table():
    orig, off = {}, 0
    for name, w in _ORIG_SEGMENTS:
        orig[name] = (off, w)
        off += w
    new, off = {}, 0
    for name in _PROJ_ORDER:
        new[name] = (off, orig[name][1])
        off += orig[name][1]
    return orig, new, off


_ORIG_OFF, _NEW_OFF, IN_WIDTH = _segment_table()


def _col_block(name, width):
    off = _NEW_OFF[name][0]
    assert off % width == 0
    return off // width


def _inproj_kernel(x_ref, g_ref, w_ref, o_ref, h_ref, *, row_chunk):
    @pl.when(pl.program_id(1) == 0)
    def _():
        def body(r, carry):
            rows = pl.ds(pl.multiple_of(r * row_chunk, row_chunk), row_chunk)
            x = x_ref[rows, :]
            ms = jnp.mean(x * x, axis=-1, keepdims=True)
            h_ref[rows, :] = ((x * lax.rsqrt(ms + NORM_EPS)) * g_ref[...]).astype(BF16)
            return carry
        lax.fori_loop(0, x_ref.shape[0] // row_chunk, body, 0)

    o_ref[...] = jnp.dot(h_ref[...], w_ref[...], preferred_element_type=F32).astype(o_ref.dtype)


def _in_projection(x2d, gain, w_bf16, *, tm=1024, tn=1536):
    n_tok, d = x2d.shape
    n_out = w_bf16.shape[1]
    assert n_tok % tm == 0 and n_out % tn == 0
    return pl.pallas_call(
        functools.partial(_inproj_kernel, row_chunk=128),
        grid=(n_tok // tm, n_out // tn),
        in_specs=[pl.BlockSpec((tm, d), lambda i, j: (i, 0)),
                  pl.BlockSpec((1, d), lambda i, j: (0, 0)),
                  pl.BlockSpec((d, tn), lambda i, j: (0, j))],
        out_specs=pl.BlockSpec((tm, tn), lambda i, j: (i, j)),
        out_shape=jax.ShapeDtypeStruct((n_tok, n_out), BF16),
        scratch_shapes=[pltpu.VMEM((tm, d), BF16)],
        compiler_params=pltpu.CompilerParams(
            dimension_semantics=("arbitrary", "arbitrary"), vmem_limit_bytes=VMEM_LIMIT),
        name="in_projection",
    )(x2d, gain.reshape(1, d), w_bf16)


def _rope_tables(pos_ref, inv_freq_row):
    n_freq = ATTN_HEAD_DIM // 2
    n_slab = LANES // n_freq
    slab = pos_ref.shape[0] // n_slab
    assert slab * n_slab == pos_ref.shape[0]
    group = lax.broadcasted_iota(jnp.int32, (1, LANES), 1) // n_freq
    packed = jnp.zeros((slab, LANES), F32)
    for i in range(n_slab):
        packed = jnp.where(group == i, pos_ref[i * slab:(i + 1) * slab, :].astype(F32), packed)
    ang = packed * inv_freq_row
    cos_p, sin_p = jnp.cos(ang), jnp.sin(ang)

    def spread(t, i):
        t = jnp.where(group == i, t, 0.0)
        t = t + pltpu.roll(t, n_freq, axis=1)
        return t + pltpu.roll(t, 2 * n_freq, axis=1)

    assert n_slab == 4
    cos = jnp.concatenate([spread(cos_p, i) for i in range(n_slab)], axis=0)
    sin = jnp.concatenate([spread(sin_p, i) for i in range(n_slab)], axis=0)
    lane = lax.broadcasted_iota(jnp.int32, (1, LANES), 1)
    first = (lane % ATTN_HEAD_DIM) < n_freq
    return cos, jnp.where(first, -sin, 0.0), jnp.where(first, 0.0, sin)


def _rope(x, cos, sin_first, sin_second):
    half = ATTN_HEAD_DIM // 2
    from_right = pltpu.roll(x, LANES - half, axis=1)
    from_left = pltpu.roll(x, half, axis=1)
    return x * cos + from_right * sin_first + from_left * sin_second


def _attn_kernel(sink_ref, invf_ref, pos_ref, q_ref, g_ref, k_ref, v_ref, o_ref, kpad_ref, vpad_ref, *, tq):
    blk = WINDOW
    nsub = tq // blk
    first_block = pl.program_id(1) == 0

    @pl.when(first_block)
    def _():
        kpad_ref[:, :blk, :] = jnp.zeros((2 * ATTN_KV_HEADS, blk, LANES), BF16)
        vpad_ref[:, :blk, :] = jnp.zeros((2 * ATTN_KV_HEADS, blk, LANES), BF16)

    @pl.when(jnp.logical_not(first_block))
    def _():
        kpad_ref[:, :blk, :] = kpad_ref[:, tq:, :]
        vpad_ref[:, :blk, :] = vpad_ref[:, tq:, :]

    cos, sin_first, sin_second = _rope_tables(pos_ref, invf_ref[...])
    lane = lax.broadcasted_iota(jnp.int32, (1, LANES), 1)
    lo_lane = lane < ATTN_HEAD_DIM

    for s in range(KV_WIDTH // LANES):
        cols = slice(s * LANES, (s + 1) * LANES)
        kk = _rope(k_ref[:, cols].astype(F32), cos, sin_first, sin_second)
        vv = v_ref[:, cols].astype(F32)
        kk_sw = pltpu.roll(kk, ATTN_HEAD_DIM, axis=1)
        vv_sw = pltpu.roll(vv, ATTN_HEAD_DIM, axis=1)
        for ref, own, swapped in ((kpad_ref, kk, kk_sw), (vpad_ref, vv, vv_sw)):
            ref[4 * s + 0, blk:, :] = jnp.where(lo_lane, own, 0.0).astype(BF16)
            ref[4 * s + 1, blk:, :] = jnp.where(lo_lane, 0.0, swapped).astype(BF16)
            ref[4 * s + 2, blk:, :] = jnp.where(lo_lane, swapped, 0.0).astype(BF16)
            ref[4 * s + 3, blk:, :] = jnp.where(lo_lane, 0.0, own).astype(BF16)

    r_idx = lax.broadcasted_iota(jnp.int32, (blk, 2 * blk), 0)
    c_idx = lax.broadcasted_iota(jnp.int32, (blk, 2 * blk), 1)
    band = (c_idx > r_idx) & (c_idx <= r_idx + blk)
    scale = (ATTN_HEAD_DIM ** -0.5) * LOG2_E
    nt = (((1,), (1,)), ((), ()))

    group_rows = ATTN_GROUP // 2 * blk
    lo_rows = lax.broadcasted_iota(jnp.int32, (group_rows, 1), 0) < blk
    band2 = jnp.concatenate([band] * (ATTN_GROUP // 2), axis=0)
    c_idx2 = jnp.concatenate([c_idx] * (ATTN_GROUP // 2), axis=0)
    for sub in range(nsub):
        rows = slice(sub * blk, (sub + 1) * blk)
        keys = slice(sub * blk, sub * blk + 2 * blk)
        if sub == 0:
            mask = band2 & (c_idx2 >= jnp.where(first_block, blk, 0))
        else:
            mask = band2
        for g in range(ATTN_KV_HEADS):
            pairs = (2 * g, 2 * g + 1)
            q2 = jnp.concatenate(
                [_rope(q_ref[rows, p * LANES:(p + 1) * LANES].astype(F32), cos[rows], sin_first[rows],
                       sin_second[rows]) for p in pairs], axis=0)
            q2 = (q2 * scale).astype(BF16)
            acc = jnp.zeros((group_rows, LANES), F32)
            inv = []
            for which in range(2):
                s = lax.dot_general(q2, kpad_ref[2 * g + which, keys, :], nt, preferred_element_type=F32)
                s = jnp.where(mask, s, -jnp.inf)
                sink = jnp.where(lo_rows, sink_ref[2 * pairs[0] + which], sink_ref[2 * pairs[1] + which]) * LOG2_E
                m = jnp.maximum(jnp.max(s, axis=-1, keepdims=True), sink)
                p = jnp.exp2(s - m)
                denom = jnp.sum(p, axis=-1, keepdims=True) + jnp.exp2(sink - m)
                inv.append(1.0 / denom)
                acc = acc + jnp.dot(p.astype(BF16), vpad_ref[2 * g + which, keys, :], preferred_element_type=F32)
            attn = acc * jnp.where(lo_lane, inv[0], inv[1])
            for i, p in enumerate(pairs):
                cols = slice(p * LANES, (p + 1) * LANES)
                gate = g_ref[rows, cols].astype(F32)
                o_ref[rows, cols] = (attn[i * blk:(i + 1) * blk] * (gate / (1.0 + jnp.exp(-gate)))).astype(o_ref.dtype)


def _attention(proj3, pos3, sinks, inv_freq_row, *, tq=512):
    b, t, _ = proj3.shape
    assert t % tq == 0 and tq % WINDOW == 0
    cur = lambda name, w: pl.BlockSpec((None, tq, w), lambda bi, i, c=_col_block(name, w): (bi, i, c))
    return pl.pallas_call(
        functools.partial(_attn_kernel, tq=tq),
        grid=(b, t // tq),
        in_specs=[pl.BlockSpec(memory_space=pltpu.SMEM),
                  pl.BlockSpec((1, LANES), lambda bi, i: (0, 0)),
                  pl.BlockSpec((None, tq, 1), lambda bi, i: (bi, i, 0)),
                  cur("aq", ATTN_WIDTH), cur("ag", ATTN_WIDTH), cur("ak", KV_WIDTH), cur("av", KV_WIDTH)],
        out_specs=pl.BlockSpec((None, tq, ATTN_WIDTH), lambda bi, i: (bi, i, 0)),
        out_shape=jax.ShapeDtypeStruct((b, t, ATTN_WIDTH), BF16),
        scratch_shapes=[pltpu.VMEM((2 * ATTN_KV_HEADS, WINDOW + tq, LANES), BF16),
                        pltpu.VMEM((2 * ATTN_KV_HEADS, WINDOW + tq, LANES), BF16)],
        compiler_params=pltpu.CompilerParams(
            dimension_semantics=("arbitrary", "arbitrary"), vmem_limit_bytes=VMEM_LIMIT),
        name="swa_attention",
    )(sinks, inv_freq_row, pos3, proj3, proj3, proj3, proj3)


def _anchor_rows(b, m):
    c, w = b.shape
    half = m // 2
    if m >= 8:
        g = c // m
        b3 = b.reshape(g, m, w)
        return jnp.broadcast_to(b3[:, half - 1:half, :], (g, m, w)).reshape(c, w)
    u = lax.broadcasted_iota(jnp.int32, (c, 1), 0) % m
    out = b
    for uu in range(m):
        shift = uu - (half - 1)
        if shift == 0:
            continue
        out = jnp.where(u == uu, pltpu.roll(b, shift % c, axis=0), out)
    return out


def _neg_abs(x):
    bits = lax.bitcast_convert_type(x, jnp.uint32) | jnp.uint32(0x80000000)
    return lax.bitcast_convert_type(bits, F32)


def _hgrn_kernel(lbp_ref, gain_ref, tril_ref, hq_ref, hf_ref, hi_ref, hg_ref, o_ref, st_ref, *, layer, chunk):
    c = chunk
    dk, dv = HGRN_KEY_DIM, HGRN_VALUE_DIM
    w = HGRN_HEADS * dk

    @pl.when(pl.program_id(1) == 0)
    def _():
        st_ref[...] = jnp.zeros_like(st_ref)

    lbp = lbp_ref[...].astype(F32)
    e = jnp.exp(lbp - jnp.max(lbp, axis=0, keepdims=True))
    lb = jnp.sum(e[:layer + 1], axis=0, keepdims=True) / jnp.sum(e, axis=0, keepdims=True)

    k = (1.0 - lb) / (1.0 + jnp.exp(hf_ref[...].astype(F32)))
    log2_f = jnp.log(1.0 - k) * LOG2_E
    k16 = k.astype(BF16)
    hq = hq_ref[...].astype(F32)
    q16 = ((hq / (1.0 + jnp.exp(-hq))) * (dk ** -0.5)).astype(BF16)
    v = hi_ref[...].astype(BF16)

    f_hi = log2_f.astype(BF16)
    f_lo = (log2_f - f_hi.astype(F32)).astype(BF16)
    cs = jnp.dot(tril_ref[...], jnp.concatenate([f_hi, f_lo], axis=1), preferred_element_type=F32)
    b = cs[:, :w] + cs[:, w:]
    b_last = b[c - 1:c, :]

    ti = lax.broadcasted_iota(jnp.int32, (c, c), 0)
    si = lax.broadcasted_iota(jnp.int32, (c, c), 1)
    txs = ti ^ si
    causal = ti > si
    nt = (((1,), (1,)), ((), ()))
    tn = (((0,), (0,)), ((), ()))

    a_mats = [None] * HGRN_HEADS
    for h in range(HGRN_HEADS):
        cols = slice(h * dk, (h + 1) * dk)
        gm = lax.dot_general(q16[:, cols], k16[:, cols], nt, preferred_element_type=F32)
        a_mats[h] = jnp.where(txs == 0, gm, 0.0)
    m = c
    while m >= 2:
        half = m // 2
        z = jnp.exp2(_neg_abs(b - _anchor_rows(b, m))).astype(BF16)
        qz = q16 * z
        kz = k16 * z
        sel = ((txs // half) == 1) & causal
        for h in range(HGRN_HEADS):
            cols = slice(h * dk, (h + 1) * dk)
            gm = lax.dot_general(qz[:, cols], kz[:, cols], nt, preferred_element_type=F32)
            a_mats[h] = jnp.where(sel, gm, a_mats[h])
        m = half

    q_dec = q16 * jnp.exp2(b).astype(BF16)
    k_dec = k16 * jnp.exp2(b_last - b).astype(BF16)
    decay_last = jnp.exp2(b_last)
    gain = gain_ref[...].astype(F32)
    hg = hg_ref[...].astype(F32)
    gate = hg / (1.0 + jnp.exp(-hg))
    for h in range(HGRN_HEADS):
        kc = slice(h * dk, (h + 1) * dk)
        vc = slice(h * dv, (h + 1) * dv)
        st = st_ref[h]
        o = jnp.dot(a_mats[h].astype(BF16), v[:, vc], preferred_element_type=F32)
        o = o + lax.dot_general(q_dec[:, kc], st.astype(BF16), nt, preferred_element_type=F32)
        st_ref[h] = st * decay_last[:, kc] + lax.dot_general(v[:, vc], k_dec[:, kc], tn,
                                                              preferred_element_type=F32)
        ms = jnp.mean(o * o, axis=-1, keepdims=True)
        y = (o * lax.rsqrt(ms + NORM_EPS)) * gain[:, vc]
        o_ref[:, vc] = (y * gate[:, vc]).astype(o_ref.dtype)


def _hgrn(proj3, lb_params, gain_row, layer, *, chunk=128):
    b, t, _ = proj3.shape
    assert t % chunk == 0 and chunk & (chunk - 1) == 0
    w = HGRN_WIDTH
    tril = jnp.asarray(np.tril(np.ones((chunk, chunk), np.float32)), dtype=BF16)
    cur = lambda name: pl.BlockSpec((None, chunk, w), lambda bi, i, c=_col_block(name, w): (bi, i, c))
    return pl.pallas_call(
        functools.partial(_hgrn_kernel, layer=layer, chunk=chunk),
        grid=(b, t // chunk),
        in_specs=[pl.BlockSpec(lb_params.shape, lambda bi, i: (0, 0)),
                  pl.BlockSpec((1, w), lambda bi, i: (0, 0)),
                  pl.BlockSpec((chunk, chunk), lambda bi, i: (0, 0)),
                  cur("hq"), cur("hf"), cur("hi"), cur("hg")],
        out_specs=pl.BlockSpec((None, chunk, w), lambda bi, i: (bi, i, 0)),
        out_shape=jax.ShapeDtypeStruct((b, t, w), BF16),
        scratch_shapes=[pltpu.VMEM((HGRN_HEADS, HGRN_VALUE_DIM, HGRN_KEY_DIM), F32)],
        compiler_params=pltpu.CompilerParams(
            dimension_semantics=("arbitrary", "arbitrary"), vmem_limit_bytes=VMEM_LIMIT),
        name="hgrn2",
    )(lb_params, gain_row, tril, proj3, proj3, proj3, proj3)


def _sigmoid(x):
    return 1.0 / (1.0 + jnp.exp(-x))


def _merge_kernel(ga_ref, gh_ref, ma_ref, mh_ref, x_ref, wa_ref, wh_ref, wo_ref, fg_ref, o_ref, *, final_norm):
    ya = jnp.dot(ga_ref[...], wa_ref[...], preferred_element_type=F32)
    yh = jnp.dot(gh_ref[...], wh_ref[...], preferred_element_type=F32)
    merged = _sigmoid(ma_ref[...].astype(F32)) * ya + _sigmoid(mh_ref[...].astype(F32)) * yh
    y = x_ref[...] + jnp.dot(merged.astype(BF16), wo_ref[...], preferred_element_type=F32)
    if final_norm:
        ms = jnp.mean(y * y, axis=-1, keepdims=True)
        y = (y * lax.rsqrt(ms + NORM_EPS)) * fg_ref[...]
    o_ref[...] = y


def _merge(ga, gh, proj2, x2d, wa, wh, wo, fgain, final_norm, *, tm=512):
    n_tok, d = x2d.shape
    assert n_tok % tm == 0
    const = lambda shape: pl.BlockSpec(shape, lambda i: (0, 0), pipeline_mode=pl.Buffered(1))
    return pl.pallas_call(
        functools.partial(_merge_kernel, final_norm=final_norm),
        grid=(n_tok // tm,),
        in_specs=[pl.BlockSpec((tm, ATTN_WIDTH), lambda i: (i, 0)),
                  pl.BlockSpec((tm, HGRN_WIDTH), lambda i: (i, 0)),
                  pl.BlockSpec((tm, d), lambda i, c=_col_block("ma", d): (i, c)),
                  pl.BlockSpec((tm, d), lambda i, c=_col_block("mh", d): (i, c)),
                  pl.BlockSpec((tm, d), lambda i: (i, 0)),
                  const(wa.shape), const(wh.shape), const(wo.shape), const((1, d))],
        out_specs=pl.BlockSpec((tm, d), lambda i: (i, 0)),
        out_shape=jax.ShapeDtypeStruct((n_tok, d), F32),
        compiler_params=pltpu.CompilerParams(
            dimension_semantics=("arbitrary",), vmem_limit_bytes=VMEM_LIMIT),
        name="merge_out",
    )(ga, gh, proj2, proj2, x2d, wa, wh, wo, fgain.reshape(1, d))


def kernel(x, positions, norm_gain, w_in, attn_sinks, hgrn_lower_bounds, hgrn_norm_gain, w_attn_out,
           w_hgrn_out, w_o, final_norm_gain):
    b, t, d = x.shape
    depth = w_in.shape[0]
    assert w_in.shape[2] == IN_WIDTH and _NEW_OFF["ma"][1] == d
    half = ATTN_HEAD_DIM // 2
    inv_freq = ROPE_THETA ** (-jnp.arange(half, dtype=F32) / half)
    inv_freq_row = jnp.tile(inv_freq, LANES // half).reshape(1, LANES)
    pos3 = positions.astype(jnp.int32).reshape(b, t, 1)

    x2d = x.reshape(b * t, d)
    for l in range(depth):
        w_l = jnp.concatenate(
            [w_in[l][:, _ORIG_OFF[n][0]:_ORIG_OFF[n][0] + _ORIG_OFF[n][1]] for n in _PROJ_ORDER],
            axis=1).astype(BF16)
        proj = _in_projection(x2d, norm_gain[l], w_l)
        proj3 = proj.reshape(b, t, IN_WIDTH)
        ga = _attention(proj3, pos3, attn_sinks[l].astype(F32), inv_freq_row)
        gh = _hgrn(proj3, hgrn_lower_bounds.astype(F32), hgrn_norm_gain[l].reshape(1, HGRN_WIDTH).astype(F32), l)
        x2d = _merge(ga.reshape(b * t, ATTN_WIDTH), gh.reshape(b * t, HGRN_WIDTH), proj, x2d,
                     w_attn_out[l].astype(BF16), w_hgrn_out[l].astype(BF16), w_o[l].astype(BF16),
                     final_norm_gain, l == depth - 1)
    return x2d.reshape(b, t, d)
```

```python
import functools

import numpy as np
import jax
import jax.numpy as jnp
from jax import lax
from jax.experimental import pallas as pl
from jax.experimental.pallas import tpu as pltpu

F32 = jnp.float32
BF16 = jnp.bfloat16

ATTN_HEADS = 16
ATTN_KV_HEADS = 4
ATTN_HEAD_DIM = 64
ATTN_GROUP = ATTN_HEADS // ATTN_KV_HEADS
ATTN_WIDTH = ATTN_HEADS * ATTN_HEAD_DIM
KV_WIDTH = ATTN_KV_HEADS * ATTN_HEAD_DIM
WINDOW = 128
ROPE_THETA = 10000.0
HGRN_HEADS = 8
HGRN_KEY_DIM = 128
HGRN_VALUE_DIM = 128
HGRN_WIDTH = HGRN_HEADS * HGRN_VALUE_DIM
NORM_EPS = 1e-6
LOG2_E = 1.4426950408889634

LANES = 128
VMEM_LIMIT = 56 * 1024 * 1024

_SEGMENTS = (("aq", 1024), ("ak", 256), ("av", 256), ("ag", 1024), ("hq", 1024),
             ("hf", 1024), ("hi", 1024), ("hg", 1024), ("ma", 2048), ("mh", 2048))
_SEG_OFF = {}
IN_WIDTH = 0
for _name, _w in _SEGMENTS:
    _SEG_OFF[_name] = (IN_WIDTH, _w)
    IN_WIDTH += _w
_COL_BLOCK = 512


def _segment_specs(name, lead_block, index_fn):
    off, width = _SEG_OFF[name]
    cb = min(_COL_BLOCK, width)
    assert off % cb == 0 and width % cb == 0
    return [pl.BlockSpec(lead_block + (cb,), lambda *ids, c=(off + i * cb) // cb: index_fn(*ids) + (c,))
            for i in range(width // cb)]


class _Cols:
    def __init__(self, refs):
        self._refs = refs
        self._cb = refs[0].shape[-1]

    def __getitem__(self, idx):
        rows, cols = idx
        start = 0 if cols.start is None else cols.start
        stop = self._cb * len(self._refs) if cols.stop is None else cols.stop
        pieces, c = [], start
        while c < stop:
            i = c // self._cb
            hi = min(stop, (i + 1) * self._cb)
            pieces.append(self._refs[i][rows, c - i * self._cb:hi - i * self._cb])
            c = hi
        return pieces[0] if len(pieces) == 1 else jnp.concatenate(pieces, axis=1)


def _inproj_kernel(x_ref, g_ref, w_ref, o_ref, h_ref, *, row_chunk):
    @pl.when(pl.program_id(1) == 0)
    def _():
        def body(r, carry):
            rows = pl.ds(pl.multiple_of(r * row_chunk, row_chunk), row_chunk)
            x = x_ref[rows, :]
            ms = jnp.mean(x * x, axis=-1, keepdims=True)
            h_ref[rows, :] = ((x * lax.rsqrt(ms + NORM_EPS)) * g_ref[...]).astype(BF16)
            return carry
        lax.fori_loop(0, x_ref.shape[0] // row_chunk, body, 0)

    o_ref[...] = jnp.dot(h_ref[...], w_ref[...], preferred_element_type=F32).astype(o_ref.dtype)


def _in_projection(x2d, gain, w_bf16, *, tm=1024, tn=1792):
    n_tok, d = x2d.shape
    n_out = w_bf16.shape[1]
    assert n_tok % tm == 0 and n_out % tn == 0
    return pl.pallas_call(
        functools.partial(_inproj_kernel, row_chunk=128),
        grid=(n_tok // tm, n_out // tn),
        in_specs=[pl.BlockSpec((tm, d), lambda i, j: (i, 0)),
                  pl.BlockSpec((1, d), lambda i, j: (0, 0)),
                  pl.BlockSpec((d, tn), lambda i, j: (0, j))],
        out_specs=pl.BlockSpec((tm, tn), lambda i, j: (i, j)),
        out_shape=jax.ShapeDtypeStruct((n_tok, n_out), BF16),
        scratch_shapes=[pltpu.VMEM((tm, d), BF16)],
        compiler_params=pltpu.CompilerParams(
            dimension_semantics=("arbitrary", "arbitrary"), vmem_limit_bytes=VMEM_LIMIT),
        name="in_projection",
    )(x2d, gain.reshape(1, d), w_bf16)


def _rope_tables(pos_ref, inv_freq_row):
    n_freq = ATTN_HEAD_DIM // 2
    n_slab = LANES // n_freq
    slab = pos_ref.shape[0] // n_slab
    assert slab * n_slab == pos_ref.shape[0]
    group = lax.broadcasted_iota(jnp.int32, (1, LANES), 1) // n_freq
    packed = jnp.zeros((slab, LANES), F32)
    for i in range(n_slab):
        packed = jnp.where(group == i, pos_ref[i * slab:(i + 1) * slab, :].astype(F32), packed)
    ang = packed * inv_freq_row
    cos_p, sin_p = jnp.cos(ang), jnp.sin(ang)

    def spread(t, i):
        t = jnp.where(group == i, t, 0.0)
        t = t + pltpu.roll(t, n_freq, axis=1)
        return t + pltpu.roll(t, 2 * n_freq, axis=1)

    assert n_slab == 4
    cos = jnp.concatenate([spread(cos_p, i) for i in range(n_slab)], axis=0)
    sin = jnp.concatenate([spread(sin_p, i) for i in range(n_slab)], axis=0)
    lane = lax.broadcasted_iota(jnp.int32, (1, LANES), 1)
    first = (lane % ATTN_HEAD_DIM) < n_freq
    return cos, jnp.where(first, -sin, 0.0), jnp.where(first, 0.0, sin)


def _rope(x, cos, sin_first, sin_second):
    half = ATTN_HEAD_DIM // 2
    from_right = pltpu.roll(x, LANES - half, axis=1)
    from_left = pltpu.roll(x, half, axis=1)
    return x * cos + from_right * sin_first + from_left * sin_second


def _attn_kernel(sink_ref, invf_ref, pos_ref, *refs, tq):
    nq = ATTN_WIDTH // _COL_BLOCK
    q_ref, g_ref = _Cols(refs[:nq]), _Cols(refs[nq:2 * nq])
    k_ref, v_ref, o_ref, kpad_ref, vpad_ref = refs[2 * nq:]
    blk = WINDOW
    nsub = tq // blk
    first_block = pl.program_id(1) == 0

    @pl.when(first_block)
    def _():
        kpad_ref[:, :blk, :] = jnp.zeros((2 * ATTN_KV_HEADS, blk, LANES), BF16)
        vpad_ref[:, :blk, :] = jnp.zeros((2 * ATTN_KV_HEADS, blk, LANES), BF16)

    @pl.when(jnp.logical_not(first_block))
    def _():
        kpad_ref[:, :blk, :] = kpad_ref[:, tq:, :]
        vpad_ref[:, :blk, :] = vpad_ref[:, tq:, :]

    cos, sin_first, sin_second = _rope_tables(pos_ref, invf_ref[...])
    lane = lax.broadcasted_iota(jnp.int32, (1, LANES), 1)
    lo_lane = lane < ATTN_HEAD_DIM

    for s in range(KV_WIDTH // LANES):
        cols = slice(s * LANES, (s + 1) * LANES)
        kk = _rope(k_ref[:, cols].astype(F32), cos, sin_first, sin_second)
        vv = v_ref[:, cols].astype(F32)
        kk_sw = pltpu.roll(kk, ATTN_HEAD_DIM, axis=1)
        vv_sw = pltpu.roll(vv, ATTN_HEAD_DIM, axis=1)
        for ref, own, swapped in ((kpad_ref, kk, kk_sw), (vpad_ref, vv, vv_sw)):
            ref[4 * s + 0, blk:, :] = jnp.where(lo_lane, own, 0.0).astype(BF16)
            ref[4 * s + 1, blk:, :] = jnp.where(lo_lane, 0.0, swapped).astype(BF16)
            ref[4 * s + 2, blk:, :] = jnp.where(lo_lane, swapped, 0.0).astype(BF16)
            ref[4 * s + 3, blk:, :] = jnp.where(lo_lane, 0.0, own).astype(BF16)

    r_idx = lax.broadcasted_iota(jnp.int32, (blk, 2 * blk), 0)
    c_idx = lax.broadcasted_iota(jnp.int32, (blk, 2 * blk), 1)
    band = (c_idx > r_idx) & (c_idx <= r_idx + blk)
    scale = (ATTN_HEAD_DIM ** -0.5) * LOG2_E
    nt = (((1,), (1,)), ((), ()))

    group_rows = ATTN_GROUP // 2 * blk
    lo_rows = lax.broadcasted_iota(jnp.int32, (group_rows, 1), 0) < blk
    band2 = jnp.concatenate([band] * (ATTN_GROUP // 2), axis=0)
    c_idx2 = jnp.concatenate([c_idx] * (ATTN_GROUP // 2), axis=0)
    for sub in range(nsub):
        rows = slice(sub * blk, (sub + 1) * blk)
        keys = slice(sub * blk, sub * blk + 2 * blk)
        if sub == 0:
            mask = band2 & (c_idx2 >= jnp.where(first_block, blk, 0))
        else:
            mask = band2
        for g in range(ATTN_KV_HEADS):
            pairs = (2 * g, 2 * g + 1)
            q2 = jnp.concatenate(
                [_rope(q_ref[rows, p * LANES:(p + 1) * LANES].astype(F32), cos[rows], sin_first[rows],
                       sin_second[rows]) for p in pairs], axis=0)
            q2 = (q2 * scale).astype(BF16)
            acc = jnp.zeros((group_rows, LANES), F32)
            inv = []
            for which in range(2):
                s = lax.dot_general(q2, kpad_ref[2 * g + which, keys, :], nt, preferred_element_type=F32)
                s = jnp.where(mask, s, -jnp.inf)
                sink = jnp.where(lo_rows, sink_ref[2 * pairs[0] + which], sink_ref[2 * pairs[1] + which]) * LOG2_E
                m = jnp.maximum(jnp.max(s, axis=-1, keepdims=True), sink)
                p = jnp.exp2(s - m)
                denom = jnp.sum(p, axis=-1, keepdims=True) + jnp.exp2(sink - m)
                inv.append(1.0 / denom)
                acc = acc + jnp.dot(p.astype(BF16), vpad_ref[2 * g + which, keys, :], preferred_element_type=F32)
            attn = acc * jnp.where(lo_lane, inv[0], inv[1])
            for i, p in enumerate(pairs):
                cols = slice(p * LANES, (p + 1) * LANES)
                gate = g_ref[rows, cols].astype(F32)
                o_ref[rows, cols] = (attn[i * blk:(i + 1) * blk] * (gate / (1.0 + jnp.exp(-gate)))).astype(o_ref.dtype)


def _attention(proj3, pos3, sinks, inv_freq_row, *, tq=512):
    b, t, _ = proj3.shape
    assert t % tq == 0 and tq % WINDOW == 0
    seg = lambda name: _segment_specs(name, (None, tq), lambda bi, i: (bi, i))
    in_specs = [pl.BlockSpec(memory_space=pltpu.SMEM),
                pl.BlockSpec((1, LANES), lambda bi, i: (0, 0)),
                pl.BlockSpec((None, tq, 1), lambda bi, i: (bi, i, 0)),
                *seg("aq"), *seg("ag"), *seg("ak"), *seg("av")]
    return pl.pallas_call(
        functools.partial(_attn_kernel, tq=tq),
        grid=(b, t // tq),
        in_specs=in_specs,
        out_specs=pl.BlockSpec((None, tq, ATTN_WIDTH), lambda bi, i: (bi, i, 0)),
        out_shape=jax.ShapeDtypeStruct((b, t, ATTN_WIDTH), BF16),
        scratch_shapes=[pltpu.VMEM((2 * ATTN_KV_HEADS, WINDOW + tq, LANES), BF16),
                        pltpu.VMEM((2 * ATTN_KV_HEADS, WINDOW + tq, LANES), BF16)],
        compiler_params=pltpu.CompilerParams(
            dimension_semantics=("arbitrary", "arbitrary"), vmem_limit_bytes=VMEM_LIMIT),
        name="swa_attention",
    )(sinks, inv_freq_row, pos3, *([proj3] * (len(in_specs) - 3)))


def _anchor_rows(b, m):
    c, w = b.shape
    half = m // 2
    if m >= 8:
        g = c // m
        b3 = b.reshape(g, m, w)
        return jnp.broadcast_to(b3[:, half - 1:half, :], (g, m, w)).reshape(c, w)
    u = lax.broadcasted_iota(jnp.int32, (c, 1), 0) % m
    out = b
    for uu in range(m):
        shift = uu - (half - 1)
        if shift == 0:
            continue
        out = jnp.where(u == uu, pltpu.roll(b, shift % c, axis=0), out)
    return out


def _neg_abs(x):
    bits = lax.bitcast_convert_type(x, jnp.uint32) | jnp.uint32(0x80000000)
    return lax.bitcast_convert_type(bits, F32)


def _hgrn_kernel(lbp_ref, gain_ref, tril_ref, *refs, layer, chunk, n_chunks):
    ns = HGRN_WIDTH // _COL_BLOCK
    hq_ref, hf_ref, hi_ref, hg_ref = (_Cols(refs[i * ns:(i + 1) * ns]) for i in range(4))
    o_ref, st_ref = refs[4 * ns:]
    c = chunk
    dk, dv = HGRN_KEY_DIM, HGRN_VALUE_DIM
    w = HGRN_HEADS * dk

    @pl.when(pl.program_id(1) == 0)
    def _():
        st_ref[...] = jnp.zeros_like(st_ref)

    lbp = lbp_ref[...].astype(F32)
    e = jnp.exp(lbp - jnp.max(lbp, axis=0, keepdims=True))
    lb = jnp.sum(e[:layer + 1], axis=0, keepdims=True) / jnp.sum(e, axis=0, keepdims=True)

    for ci in range(n_chunks):
        rows = slice(ci * c, (ci + 1) * c)
        k = (1.0 - lb) / (1.0 + jnp.exp(hf_ref[rows, :].astype(F32)))
        log2_f = jnp.log(1.0 - k) * LOG2_E
        k16 = k.astype(BF16)
        hq = hq_ref[rows, :].astype(F32)
        q16 = ((hq / (1.0 + jnp.exp(-hq))) * (dk ** -0.5)).astype(BF16)
        v = hi_ref[rows, :]

        f_hi = log2_f.astype(BF16)
        f_lo = (log2_f - f_hi.astype(F32)).astype(BF16)
        cs = jnp.dot(tril_ref[...], jnp.concatenate([f_hi, f_lo], axis=1), preferred_element_type=F32)
        b = cs[:, :w] + cs[:, w:]
        b_last = b[c - 1:c, :]

        ti = lax.broadcasted_iota(jnp.int32, (c, c), 0)
        si = lax.broadcasted_iota(jnp.int32, (c, c), 1)
        txs = ti ^ si
        causal = ti > si
        nt = (((1,), (1,)), ((), ()))
        tn = (((0,), (0,)), ((), ()))

        a_mats = [None] * HGRN_HEADS
        for h in range(HGRN_HEADS):
            cols = slice(h * dk, (h + 1) * dk)
            gm = lax.dot_general(q16[:, cols], k16[:, cols], nt, preferred_element_type=F32)
            a_mats[h] = jnp.where(txs == 0, gm, 0.0)
        m = c
        while m >= 2:
            half = m // 2
            z = jnp.exp2(_neg_abs(b - _anchor_rows(b, m))).astype(BF16)
            qz = q16 * z
            kz = k16 * z
            sel = ((txs // half) == 1) & causal
            for h in range(HGRN_HEADS):
                cols = slice(h * dk, (h + 1) * dk)
                gm = lax.dot_general(qz[:, cols], kz[:, cols], nt, preferred_element_type=F32)
                a_mats[h] = jnp.where(sel, gm, a_mats[h])
            m = half

        q_dec = q16 * jnp.exp2(b).astype(BF16)
        k_dec = k16 * jnp.exp2(b_last - b).astype(BF16)
        decay_last = jnp.exp2(b_last)
        gain = gain_ref[...].astype(F32)
        hg = hg_ref[rows, :].astype(F32)
        gate = hg / (1.0 + jnp.exp(-hg))
        for h in range(HGRN_HEADS):
            kc = slice(h * dk, (h + 1) * dk)
            vc = slice(h * dv, (h + 1) * dv)
            st = st_ref[h]
            o = jnp.dot(a_mats[h].astype(BF16), v[:, vc], preferred_element_type=F32)
            o = o + lax.dot_general(q_dec[:, kc], st.astype(BF16), nt, preferred_element_type=F32)
            st_ref[h] = st * decay_last[:, kc] + lax.dot_general(v[:, vc], k_dec[:, kc], tn,
                                                                  preferred_element_type=F32)
            ms = jnp.mean(o * o, axis=-1, keepdims=True)
            y = (o * lax.rsqrt(ms + NORM_EPS)) * gain[:, vc]
            o_ref[rows, vc] = (y * gate[:, vc]).astype(o_ref.dtype)


def _hgrn(proj3, lb_params, gain_row, layer, *, chunk=128, n_chunks=4):
    b, t, _ = proj3.shape
    rows = chunk * n_chunks
    assert t % rows == 0 and chunk & (chunk - 1) == 0
    w = HGRN_WIDTH
    tril = jnp.asarray(np.tril(np.ones((chunk, chunk), np.float32)), dtype=BF16)
    seg = lambda name: _segment_specs(name, (None, rows), lambda bi, i: (bi, i))
    in_specs = [pl.BlockSpec(lb_params.shape, lambda bi, i: (0, 0)),
                pl.BlockSpec((1, w), lambda bi, i: (0, 0)),
                pl.BlockSpec((chunk, chunk), lambda bi, i: (0, 0)),
                *seg("hq"), *seg("hf"), *seg("hi"), *seg("hg")]
    return pl.pallas_call(
        functools.partial(_hgrn_kernel, layer=layer, chunk=chunk, n_chunks=n_chunks),
        grid=(b, t // rows),
        in_specs=in_specs,
        out_specs=pl.BlockSpec((None, rows, w), lambda bi, i: (bi, i, 0)),
        out_shape=jax.ShapeDtypeStruct((b, t, w), BF16),
        scratch_shapes=[pltpu.VMEM((HGRN_HEADS, HGRN_VALUE_DIM, HGRN_KEY_DIM), F32)],
        compiler_params=pltpu.CompilerParams(
            dimension_semantics=("arbitrary", "arbitrary"), vmem_limit_bytes=VMEM_LIMIT),
        name="hgrn2",
    )(lb_params, gain_row, tril, *([proj3] * (len(in_specs) - 3)))


def _sigmoid(x):
    return 1.0 / (1.0 + jnp.exp(-x))


def _merge_kernel(ga_ref, gh_ref, *refs, final_norm):
    ng = (len(refs) - 6) // 2
    ma_ref, mh_ref = _Cols(refs[:ng]), _Cols(refs[ng:2 * ng])
    x_ref, wa_ref, wh_ref, wo_ref, fg_ref, o_ref = refs[2 * ng:]
    ya = jnp.dot(ga_ref[...], wa_ref[...], preferred_element_type=F32)
    yh = jnp.dot(gh_ref[...], wh_ref[...], preferred_element_type=F32)
    merged = _sigmoid(ma_ref[:, :].astype(F32)) * ya + _sigmoid(mh_ref[:, :].astype(F32)) * yh
    y = x_ref[...] + jnp.dot(merged.astype(BF16), wo_ref[...], preferred_element_type=F32)
    if final_norm:
        ms = jnp.mean(y * y, axis=-1, keepdims=True)
        y = (y * lax.rsqrt(ms + NORM_EPS)) * fg_ref[...]
    o_ref[...] = y


def _merge(ga, gh, proj2, x2d, wa, wh, wo, fgain, final_norm, *, tm=512):
    n_tok, d = x2d.shape
    assert n_tok % tm == 0
    const = lambda shape: pl.BlockSpec(shape, lambda i: (0, 0), pipeline_mode=pl.Buffered(1))
    seg = lambda name: _segment_specs(name, (tm,), lambda i: (i,))
    gate_specs = [*seg("ma"), *seg("mh")]
    return pl.pallas_call(
        functools.partial(_merge_kernel, final_norm=final_norm),
        grid=(n_tok // tm,),
        in_specs=[pl.BlockSpec((tm, ATTN_WIDTH), lambda i: (i, 0)),
                  pl.BlockSpec((tm, HGRN_WIDTH), lambda i: (i, 0)),
                  *gate_specs,
                  pl.BlockSpec((tm, d), lambda i: (i, 0)),
                  const(wa.shape), const(wh.shape), const(wo.shape), const((1, d))],
        out_specs=pl.BlockSpec((tm, d), lambda i: (i, 0)),
        out_shape=jax.ShapeDtypeStruct((n_tok, d), F32),
        compiler_params=pltpu.CompilerParams(
            dimension_semantics=("arbitrary",), vmem_limit_bytes=VMEM_LIMIT),
        name="merge_out",
    )(ga, gh, *([proj2] * len(gate_specs)), x2d, wa, wh, wo, fgain.reshape(1, d))


def kernel(x, positions, norm_gain, w_in, attn_sinks, hgrn_lower_bounds, hgrn_norm_gain, w_attn_out,
           w_hgrn_out, w_o, final_norm_gain):
    b, t, d = x.shape
    depth = w_in.shape[0]
    assert w_in.shape[2] == IN_WIDTH and _SEG_OFF["ma"][1] == d
    half = ATTN_HEAD_DIM // 2
    inv_freq = ROPE_THETA ** (-jnp.arange(half, dtype=F32) / half)
    inv_freq_row = jnp.tile(inv_freq, LANES // half).reshape(1, LANES)
    pos3 = positions.astype(jnp.int32).reshape(b, t, 1)

    x2d = x.reshape(b * t, d)
    for l in range(depth):
        proj = _in_projection(x2d, norm_gain[l], w_in[l].astype(BF16))
        proj3 = proj.reshape(b, t, IN_WIDTH)
        ga = _attention(proj3, pos3, attn_sinks[l].astype(F32), inv_freq_row)
        gh = _hgrn(proj3, hgrn_lower_bounds.astype(F32), hgrn_norm_gain[l].reshape(1, HGRN_WIDTH).astype(F32), l)
        x2d = _merge(ga.reshape(b * t, ATTN_WIDTH), gh.reshape(b * t, HGRN_WIDTH), proj, x2d,
                     w_attn_out[l].astype(BF16), w_hgrn_out[l].astype(BF16), w_o[l].astype(BF16),
                     final_norm_gain, l == depth - 1)
    return x2d.reshape(b, t, d)
```

```python
import functools

import numpy as np
import jax
import jax.numpy as jnp
from jax import lax
from jax.experimental import pallas as pl
from jax.experimental.pallas import tpu as pltpu

F32 = jnp.float32
BF16 = jnp.bfloat16

ATTN_HEADS = 16
ATTN_KV_HEADS = 4
ATTN_HEAD_DIM = 64
ATTN_GROUP = ATTN_HEADS // ATTN_KV_HEADS
ATTN_WIDTH = ATTN_HEADS * ATTN_HEAD_DIM
KV_WIDTH = ATTN_KV_HEADS * ATTN_HEAD_DIM
WINDOW = 128
ROPE_THETA = 10000.0
HGRN_HEADS = 8
HGRN_KEY_DIM = 128
HGRN_VALUE_DIM = 128
HGRN_WIDTH = HGRN_HEADS * HGRN_VALUE_DIM
NORM_EPS = 1e-6
LOG2_E = 1.4426950408889634

LANES = 128
VMEM_LIMIT = 56 * 1024 * 1024

_SEGMENTS = (("aq", 1024), ("ak", 256), ("av", 256), ("ag", 1024), ("hq", 1024),
             ("hf", 1024), ("hi", 1024), ("hg", 1024), ("ma", 2048), ("mh", 2048))
_SEG_OFF = {}
IN_WIDTH = 0
for _name, _w in _SEGMENTS:
    _SEG_OFF[_name] = (IN_WIDTH, _w)
    IN_WIDTH += _w
_COL_BLOCK = 512


def _segment_specs(name, lead_block, index_fn):
    off, width = _SEG_OFF[name]
    cb = min(_COL_BLOCK, width)
    assert off % cb == 0 and width % cb == 0
    return [pl.BlockSpec(lead_block + (cb,), lambda *ids, c=(off + i * cb) // cb: index_fn(*ids) + (c,))
            for i in range(width // cb)]


class _Cols:
    def __init__(self, refs):
        self._refs = refs
        self._cb = refs[0].shape[-1]

    def __getitem__(self, idx):
        rows, cols = idx
        start = 0 if cols.start is None else cols.start
        stop = self._cb * len(self._refs) if cols.stop is None else cols.stop
        pieces, c = [], start
        while c < stop:
            i = c // self._cb
            hi = min(stop, (i + 1) * self._cb)
            pieces.append(self._refs[i][rows, c - i * self._cb:hi - i * self._cb])
            c = hi
        return pieces[0] if len(pieces) == 1 else jnp.concatenate(pieces, axis=1)


def _inproj_kernel(x_ref, g_ref, w_ref, o_ref, h_ref, *, row_chunk):
    @pl.when(pl.program_id(1) == 0)
    def _():
        def body(r, carry):
            rows = pl.ds(pl.multiple_of(r * row_chunk, row_chunk), row_chunk)
            x = x_ref[rows, :]
            ms = jnp.mean(x * x, axis=-1, keepdims=True)
            h_ref[rows, :] = ((x * lax.rsqrt(ms + NORM_EPS)) * g_ref[...]).astype(BF16)
            return carry
        lax.fori_loop(0, x_ref.shape[0] // row_chunk, body, 0)

    o_ref[...] = jnp.dot(h_ref[...], w_ref[...], preferred_element_type=F32).astype(o_ref.dtype)


def _in_projection(x2d, gain, w_bf16, *, tm=1024, tn=1792):
    n_tok, d = x2d.shape
    n_out = w_bf16.shape[1]
    assert n_tok % tm == 0 and n_out % tn == 0
    return pl.pallas_call(
        functools.partial(_inproj_kernel, row_chunk=128),
        grid=(n_tok // tm, n_out // tn),
        in_specs=[pl.BlockSpec((tm, d), lambda i, j: (i, 0)),
                  pl.BlockSpec((1, d), lambda i, j: (0, 0)),
                  pl.BlockSpec((d, tn), lambda i, j: (0, j))],
        out_specs=pl.BlockSpec((tm, tn), lambda i, j: (i, j)),
        out_shape=jax.ShapeDtypeStruct((n_tok, n_out), BF16),
        scratch_shapes=[pltpu.VMEM((tm, d), BF16)],
        compiler_params=pltpu.CompilerParams(
            dimension_semantics=("arbitrary", "arbitrary"), vmem_limit_bytes=VMEM_LIMIT),
        name="in_projection",
    )(x2d, gain.reshape(1, d), w_bf16)


def _rope_tables(pos_ref, inv_freq_row):
    n_freq = ATTN_HEAD_DIM // 2
    n_slab = LANES // n_freq
    slab = pos_ref.shape[0] // n_slab
    assert slab * n_slab == pos_ref.shape[0]
    group = lax.broadcasted_iota(jnp.int32, (1, LANES), 1) // n_freq
    packed = jnp.zeros((slab, LANES), F32)
    for i in range(n_slab):
        packed = jnp.where(group == i, pos_ref[i * slab:(i + 1) * slab, :].astype(F32), packed)
    ang = packed * inv_freq_row
    cos_p, sin_p = jnp.cos(ang), jnp.sin(ang)

    def spread(t, i):
        t = jnp.where(group == i, t, 0.0)
        t = t + pltpu.roll(t, n_freq, axis=1)
        return t + pltpu.roll(t, 2 * n_freq, axis=1)

    assert n_slab == 4
    cos = jnp.concatenate([spread(cos_p, i) for i in range(n_slab)], axis=0)
    sin = jnp.concatenate([spread(sin_p, i) for i in range(n_slab)], axis=0)
    lane = lax.broadcasted_iota(jnp.int32, (1, LANES), 1)
    first = (lane % ATTN_HEAD_DIM) < n_freq
    return cos, jnp.where(first, -sin, 0.0), jnp.where(first, 0.0, sin)


def _rope(x, cos, sin_first, sin_second):
    half = ATTN_HEAD_DIM // 2
    from_right = pltpu.roll(x, LANES - half, axis=1)
    from_left = pltpu.roll(x, half, axis=1)
    return x * cos + from_right * sin_first + from_left * sin_second


def _attn_kernel(sink_ref, invf_ref, pos_ref, *refs, tq):
    nq = ATTN_WIDTH // _COL_BLOCK
    q_ref, g_ref = _Cols(refs[:nq]), _Cols(refs[nq:2 * nq])
    k_ref, v_ref, o_ref, kpad_ref, vpad_ref = refs[2 * nq:]
    blk = WINDOW
    nsub = tq // blk
    first_block = pl.program_id(1) == 0

    @pl.when(first_block)
    def _():
        kpad_ref[:, :blk, :] = jnp.zeros((2 * ATTN_KV_HEADS, blk, LANES), BF16)
        vpad_ref[:, :blk, :] = jnp.zeros((2 * ATTN_KV_HEADS, blk, LANES), BF16)

    @pl.when(jnp.logical_not(first_block))
    def _():
        kpad_ref[:, :blk, :] = kpad_ref[:, tq:, :]
        vpad_ref[:, :blk, :] = vpad_ref[:, tq:, :]

    cos, sin_first, sin_second = _rope_tables(pos_ref, invf_ref[...])
    lane = lax.broadcasted_iota(jnp.int32, (1, LANES), 1)
    lo_lane = lane < ATTN_HEAD_DIM

    for s in range(KV_WIDTH // LANES):
        cols = slice(s * LANES, (s + 1) * LANES)
        kk = _rope(k_ref[:, cols].astype(F32), cos, sin_first, sin_second)
        vv = v_ref[:, cols].astype(F32)
        kk_sw = pltpu.roll(kk, ATTN_HEAD_DIM, axis=1)
        vv_sw = pltpu.roll(vv, ATTN_HEAD_DIM, axis=1)
        for ref, own, swapped in ((kpad_ref, kk, kk_sw), (vpad_ref, vv, vv_sw)):
            ref[4 * s + 0, blk:, :] = jnp.where(lo_lane, own, 0.0).astype(BF16)
            ref[4 * s + 1, blk:, :] = jnp.where(lo_lane, 0.0, swapped).astype(BF16)
            ref[4 * s + 2, blk:, :] = jnp.where(lo_lane, swapped, 0.0).astype(BF16)
            ref[4 * s + 3, blk:, :] = jnp.where(lo_lane, 0.0, own).astype(BF16)

    r_idx = lax.broadcasted_iota(jnp.int32, (blk, 2 * blk), 0)
    c_idx = lax.broadcasted_iota(jnp.int32, (blk, 2 * blk), 1)
    band = (c_idx > r_idx) & (c_idx <= r_idx + blk)
    scale = (ATTN_HEAD_DIM ** -0.5) * LOG2_E
    nt = (((1,), (1,)), ((), ()))

    group_rows = ATTN_GROUP // 2 * blk
    lo_rows = lax.broadcasted_iota(jnp.int32, (group_rows, 1), 0) < blk
    band2 = jnp.concatenate([band] * (ATTN_GROUP // 2), axis=0)
    c_idx2 = jnp.concatenate([c_idx] * (ATTN_GROUP // 2), axis=0)
    for sub in range(nsub):
        rows = slice(sub * blk, (sub + 1) * blk)
        keys = slice(sub * blk, sub * blk + 2 * blk)
        if sub == 0:
            mask = band2 & (c_idx2 >= jnp.where(first_block, blk, 0))
        else:
            mask = band2
        for g in range(ATTN_KV_HEADS):
            pairs = (2 * g, 2 * g + 1)
            q2 = jnp.concatenate(
                [_rope(q_ref[rows, p * LANES:(p + 1) * LANES].astype(F32), cos[rows], sin_first[rows],
                       sin_second[rows]) for p in pairs], axis=0)
            q2 = (q2 * scale).astype(BF16)
            acc = jnp.zeros((group_rows, LANES), F32)
            inv = []
            for which in range(2):
                s = lax.dot_general(q2, kpad_ref[2 * g + which, keys, :], nt, preferred_element_type=F32)
                s = jnp.where(mask, s, -jnp.inf)
                sink = jnp.where(lo_rows, sink_ref[2 * pairs[0] + which], sink_ref[2 * pairs[1] + which]) * LOG2_E
                m = jnp.maximum(jnp.max(s, axis=-1, keepdims=True), sink)
                p = jnp.exp2(s - m)
                denom = jnp.sum(p, axis=-1, keepdims=True) + jnp.exp2(sink - m)
                inv.append(1.0 / denom)
                acc = acc + jnp.dot(p.astype(BF16), vpad_ref[2 * g + which, keys, :], preferred_element_type=F32)
            attn = acc * jnp.where(lo_lane, inv[0], inv[1])
            for i, p in enumerate(pairs):
                cols = slice(p * LANES, (p + 1) * LANES)
                gate = g_ref[rows, cols].astype(F32)
                o_ref[rows, cols] = (attn[i * blk:(i + 1) * blk] * (gate / (1.0 + jnp.exp(-gate)))).astype(o_ref.dtype)


def _attention(proj3, pos3, sinks, inv_freq_row, *, tq=512):
    b, t, _ = proj3.shape
    assert t % tq == 0 and tq % WINDOW == 0
    seg = lambda name: _segment_specs(name, (None, tq), lambda bi, i: (bi, i))
    in_specs = [pl.BlockSpec(memory_space=pltpu.SMEM),
                pl.BlockSpec((1, LANES), lambda bi, i: (0, 0)),
                pl.BlockSpec((None, tq, 1), lambda bi, i: (bi, i, 0)),
                *seg("aq"), *seg("ag"), *seg("ak"), *seg("av")]
    return pl.pallas_call(
        functools.partial(_attn_kernel, tq=tq),
        grid=(b, t // tq),
        in_specs=in_specs,
        out_specs=pl.BlockSpec((None, tq, ATTN_WIDTH), lambda bi, i: (bi, i, 0)),
        out_shape=jax.ShapeDtypeStruct((b, t, ATTN_WIDTH), BF16),
        scratch_shapes=[pltpu.VMEM((2 * ATTN_KV_HEADS, WINDOW + tq, LANES), BF16),
                        pltpu.VMEM((2 * ATTN_KV_HEADS, WINDOW + tq, LANES), BF16)],
        compiler_params=pltpu.CompilerParams(
            dimension_semantics=("arbitrary", "arbitrary"), vmem_limit_bytes=VMEM_LIMIT),
        name="swa_attention",
    )(sinks, inv_freq_row, pos3, *([proj3] * (len(in_specs) - 3)))


def _anchor_rows(b, m):
    c, w = b.shape
    g = c // m
    b3 = b.reshape(g, m, w)
    return jnp.broadcast_to(b3[:, m // 2 - 1:m // 2, :], (g, m, w)).reshape(c, w)


def _hgrn_kernel(lbp_ref, gain_ref, tril_ref, *refs, layer, chunk, n_chunks):
    ns = HGRN_WIDTH // _COL_BLOCK
    hq_ref, hf_ref, hi_ref, hg_ref = (_Cols(refs[i * ns:(i + 1) * ns]) for i in range(4))
    o_ref, st_ref = refs[4 * ns:]
    c = chunk
    dk, dv = HGRN_KEY_DIM, HGRN_VALUE_DIM
    w = HGRN_HEADS * dk

    @pl.when(pl.program_id(1) == 0)
    def _():
        st_ref[...] = jnp.zeros_like(st_ref)

    lbp = lbp_ref[...].astype(F32)
    e = jnp.exp(lbp - jnp.max(lbp, axis=0, keepdims=True))
    lb = jnp.sum(e[:layer + 1], axis=0, keepdims=True) / jnp.sum(e, axis=0, keepdims=True)

    for ci in range(n_chunks):
        rows = slice(ci * c, (ci + 1) * c)
        k = (1.0 - lb) / (1.0 + jnp.exp(hf_ref[rows, :].astype(F32)))
        log2_f = jnp.log(1.0 - k) * LOG2_E
        k16 = k.astype(BF16)
        hq = hq_ref[rows, :].astype(F32)
        q16 = ((hq / (1.0 + jnp.exp(-hq))) * (dk ** -0.5)).astype(BF16)
        v = hi_ref[rows, :]

        f_hi = log2_f.astype(BF16)
        f_lo = (log2_f - f_hi.astype(F32)).astype(BF16)
        cs = jnp.dot(tril_ref[...], jnp.concatenate([f_hi, f_lo], axis=1), preferred_element_type=F32)
        b = cs[:, :w] + cs[:, w:]
        b_last = b[c - 1:c, :]

        ti = lax.broadcasted_iota(jnp.int32, (c, c), 0)
        si = lax.broadcasted_iota(jnp.int32, (c, c), 1)
        txs = ti ^ si
        causal = ti > si
        nt = (((1,), (1,)), ((), ()))
        tn = (((0,), (0,)), ((), ()))

        a_mats = [None] * HGRN_HEADS
        for h in range(HGRN_HEADS):
            cols = slice(h * dk, (h + 1) * dk)
            gm = lax.dot_general(q16[:, cols], k16[:, cols], nt, preferred_element_type=F32)
            a_mats[h] = jnp.where(txs == 0, gm, 0.0)
        f = 1.0 - k
        row = lax.broadcasted_iota(jnp.int32, (c, 1), 0)
        m = c
        while m >= 2:
            half = m // 2
            if m > 4:
                z = jnp.exp2(-jnp.abs(b - _anchor_rows(b, m)))
            elif m == 4:
                u = row % 4
                z = jnp.where(u == 0, pltpu.roll(f, c - 1, axis=0),
                              jnp.where(u == 1, 1.0, jnp.where(u == 2, f, f * pltpu.roll(f, 1, axis=0))))
            else:
                z = jnp.where(row % 2 == 1, f, 1.0)
            z = z.astype(BF16)
            qz = q16 * z
            kz = k16 * z
            sel = ((txs // half) == 1) & causal
            for h in range(HGRN_HEADS):
                cols = slice(h * dk, (h + 1) * dk)
                gm = lax.dot_general(qz[:, cols], kz[:, cols], nt, preferred_element_type=F32)
                a_mats[h] = jnp.where(sel, gm, a_mats[h])
            m = half

        q_dec = q16 * jnp.exp2(b).astype(BF16)
        k_dec = k16 * jnp.exp2(b_last - b).astype(BF16)
        decay_last = jnp.exp2(b_last)
        gain = gain_ref[...].astype(F32)
        hg = hg_ref[rows, :].astype(F32)
        gate = hg / (1.0 + jnp.exp(-hg))
        for h in range(HGRN_HEADS):
            kc = slice(h * dk, (h + 1) * dk)
            vc = slice(h * dv, (h + 1) * dv)
            st = st_ref[h]
            o = jnp.dot(a_mats[h].astype(BF16), v[:, vc], preferred_element_type=F32)
            o = o + lax.dot_general(q_dec[:, kc], st.astype(BF16), nt, preferred_element_type=F32)
            st_ref[h] = st * decay_last[:, kc] + lax.dot_general(v[:, vc], k_dec[:, kc], tn,
                                                                  preferred_element_type=F32)
            ms = jnp.mean(o * o, axis=-1, keepdims=True)
            y = (o * lax.rsqrt(ms + NORM_EPS)) * gain[:, vc]
            o_ref[rows, vc] = (y * gate[:, vc]).astype(o_ref.dtype)


def _hgrn(proj3, lb_params, gain_row, layer, *, chunk=128, n_chunks=4):
    b, t, _ = proj3.shape
    rows = chunk * n_chunks
    assert t % rows == 0 and chunk & (chunk - 1) == 0
    w = HGRN_WIDTH
    tril = jnp.asarray(np.tril(np.ones((chunk, chunk), np.float32)), dtype=BF16)
    seg = lambda name: _segment_specs(name, (None, rows), lambda bi, i: (bi, i))
    in_specs = [pl.BlockSpec(lb_params.shape, lambda bi, i: (0, 0)),
                pl.BlockSpec((1, w), lambda bi, i: (0, 0)),
                pl.BlockSpec((chunk, chunk), lambda bi, i: (0, 0)),
                *seg("hq"), *seg("hf"), *seg("hi"), *seg("hg")]
    return pl.pallas_call(
        functools.partial(_hgrn_kernel, layer=layer, chunk=chunk, n_chunks=n_chunks),
        grid=(b, t // rows),
        in_specs=in_specs,
        out_specs=pl.BlockSpec((None, rows, w), lambda bi, i: (bi, i, 0)),
        out_shape=jax.ShapeDtypeStruct((b, t, w), BF16),
        scratch_shapes=[pltpu.VMEM((HGRN_HEADS, HGRN_VALUE_DIM, HGRN_KEY_DIM), F32)],
        compiler_params=pltpu.CompilerParams(
            dimension_semantics=("arbitrary", "arbitrary"), vmem_limit_bytes=VMEM_LIMIT),
        name="hgrn2",
    )(lb_params, gain_row, tril, *([proj3] * (len(in_specs) - 3)))


def _sigmoid(x):
    return 1.0 / (1.0 + jnp.exp(-x))


def _merge_kernel(ga_ref, gh_ref, *refs, final_norm):
    ng = (len(refs) - 6) // 2
    ma_ref, mh_ref = _Cols(refs[:ng]), _Cols(refs[ng:2 * ng])
    x_ref, wa_ref, wh_ref, wo_ref, fg_ref, o_ref = refs[2 * ng:]
    tm = o_ref.shape[0]
    halves = [slice(0, tm // 2), slice(tm // 2, tm)]

    def branches(r):
        ya = jnp.dot(ga_ref[r, :], wa_ref[...], preferred_element_type=F32)
        yh = jnp.dot(gh_ref[r, :], wh_ref[...], preferred_element_type=F32)
        return ya, yh

    def merge(r, ya, yh):
        return (_sigmoid(ma_ref[r, :].astype(F32)) * ya + _sigmoid(mh_ref[r, :].astype(F32)) * yh).astype(BF16)

    def project(r, merged):
        return x_ref[r, :] + jnp.dot(merged, wo_ref[...], preferred_element_type=F32)

    def finish(r, y):
        if final_norm:
            ms = jnp.mean(y * y, axis=-1, keepdims=True)
            y = (y * lax.rsqrt(ms + NORM_EPS)) * fg_ref[...]
        o_ref[r, :] = y

    b0 = branches(halves[0])
    b1 = branches(halves[1])
    m0 = merge(halves[0], *b0)
    y0 = project(halves[0], m0)
    m1 = merge(halves[1], *b1)
    finish(halves[0], y0)
    y1 = project(halves[1], m1)
    finish(halves[1], y1)


def _merge(ga, gh, proj2, x2d, wa, wh, wo, fgain, final_norm, *, tm=512):
    n_tok, d = x2d.shape
    assert n_tok % tm == 0
    const = lambda shape: pl.BlockSpec(shape, lambda i: (0, 0), pipeline_mode=pl.Buffered(1))
    seg = lambda name: _segment_specs(name, (tm,), lambda i: (i,))
    gate_specs = [*seg("ma"), *seg("mh")]
    return pl.pallas_call(
        functools.partial(_merge_kernel, final_norm=final_norm),
        grid=(n_tok // tm,),
        in_specs=[pl.BlockSpec((tm, ATTN_WIDTH), lambda i: (i, 0)),
                  pl.BlockSpec((tm, HGRN_WIDTH), lambda i: (i, 0)),
                  *gate_specs,
                  pl.BlockSpec((tm, d), lambda i: (i, 0)),
                  const(wa.shape), const(wh.shape), const(wo.shape), const((1, d))],
        out_specs=pl.BlockSpec((tm, d), lambda i: (i, 0)),
        out_shape=jax.ShapeDtypeStruct((n_tok, d), F32),
        compiler_params=pltpu.CompilerParams(
            dimension_semantics=("arbitrary",), vmem_limit_bytes=VMEM_LIMIT),
        name="merge_out",
    )(ga, gh, *([proj2] * len(gate_specs)), x2d, wa, wh, wo, fgain.reshape(1, d))


def kernel(x, positions, norm_gain, w_in, attn_sinks, hgrn_lower_bounds, hgrn_norm_gain, w_attn_out,
           w_hgrn_out, w_o, final_norm_gain):
    b, t, d = x.shape
    depth = w_in.shape[0]
    assert w_in.shape[2] == IN_WIDTH and _SEG_OFF["ma"][1] == d
    half = ATTN_HEAD_DIM // 2
    inv_freq = ROPE_THETA ** (-jnp.arange(half, dtype=F32) / half)
    inv_freq_row = jnp.tile(inv_freq, LANES // half).reshape(1, LANES)
    pos3 = positions.astype(jnp.int32).reshape(b, t, 1)

    x2d = x.reshape(b * t, d)
    for l in range(depth):
        proj = _in_projection(x2d, norm_gain[l], w_in[l].astype(BF16))
        proj3 = proj.reshape(b, t, IN_WIDTH)
        ga = _attention(proj3, pos3, attn_sinks[l].astype(F32), inv_freq_row)
        gh = _hgrn(proj3, hgrn_lower_bounds.astype(F32), hgrn_norm_gain[l].reshape(1, HGRN_WIDTH).astype(F32), l)
        x2d = _merge(ga.reshape(b * t, ATTN_WIDTH), gh.reshape(b * t, HGRN_WIDTH), proj, x2d,
                     w_attn_out[l].astype(BF16), w_hgrn_out[l].astype(BF16), w_o[l].astype(BF16),
                     final_norm_gain, l == depth - 1)
    return x2d.reshape(b, t, d)
```

```python
import functools

import numpy as np
import jax
import jax.numpy as jnp
from jax import lax
from jax.experimental import pallas as pl
from jax.experimental.pallas import tpu as pltpu

F32 = jnp.float32
BF16 = jnp.bfloat16

ATTN_HEADS = 16
ATTN_KV_HEADS = 4
ATTN_HEAD_DIM = 64
ATTN_GROUP = ATTN_HEADS // ATTN_KV_HEADS
ATTN_WIDTH = ATTN_HEADS * ATTN_HEAD_DIM
KV_WIDTH = ATTN_KV_HEADS * ATTN_HEAD_DIM
WINDOW = 128
ROPE_THETA = 10000.0
HGRN_HEADS = 8
HGRN_KEY_DIM = 128
HGRN_VALUE_DIM = 128
HGRN_WIDTH = HGRN_HEADS * HGRN_VALUE_DIM
NORM_EPS = 1e-6
LOG2_E = 1.4426950408889634

LANES = 128
VMEM_LIMIT = 56 * 1024 * 1024

_SEGMENTS = (("aq", 1024), ("ak", 256), ("av", 256), ("ag", 1024), ("hq", 1024),
             ("hf", 1024), ("hi", 1024), ("hg", 1024), ("ma", 2048), ("mh", 2048))
_SEG_OFF = {}
IN_WIDTH = 0
for _name, _w in _SEGMENTS:
    _SEG_OFF[_name] = (IN_WIDTH, _w)
    IN_WIDTH += _w
_COL_BLOCK = 512


def _segment_specs(name, lead_block, index_fn):
    off, width = _SEG_OFF[name]
    cb = min(_COL_BLOCK, width)
    assert off % cb == 0 and width % cb == 0
    return [pl.BlockSpec(lead_block + (cb,), lambda *ids, c=(off + i * cb) // cb: index_fn(*ids) + (c,))
            for i in range(width // cb)]


class _Cols:
    def __init__(self, refs):
        self._refs = refs
        self._cb = refs[0].shape[-1]

    def __getitem__(self, idx):
        rows, cols = idx
        start = 0 if cols.start is None else cols.start
        stop = self._cb * len(self._refs) if cols.stop is None else cols.stop
        pieces, c = [], start
        while c < stop:
            i = c // self._cb
            hi = min(stop, (i + 1) * self._cb)
            pieces.append(self._refs[i][rows, c - i * self._cb:hi - i * self._cb])
            c = hi
        return pieces[0] if len(pieces) == 1 else jnp.concatenate(pieces, axis=1)


def _inproj_kernel(x_ref, g_ref, w_ref, o_ref, h_ref, *, row_chunk):
    @pl.when(pl.program_id(1) == 0)
    def _():
        def body(r, carry):
            rows = pl.ds(pl.multiple_of(r * row_chunk, row_chunk), row_chunk)
            x = x_ref[rows, :]
            ms = jnp.mean(x * x, axis=-1, keepdims=True)
            h_ref[rows, :] = ((x * lax.rsqrt(ms + NORM_EPS)) * g_ref[...]).astype(BF16)
            return carry
        lax.fori_loop(0, x_ref.shape[0] // row_chunk, body, 0)

    o_ref[...] = jnp.dot(h_ref[...], w_ref[...], preferred_element_type=F32).astype(o_ref.dtype)


def _in_projection(x2d, gain, w_bf16, *, tm=1024, tn=1792):
    n_tok, d = x2d.shape
    n_out = w_bf16.shape[1]
    assert n_tok % tm == 0 and n_out % tn == 0
    return pl.pallas_call(
        functools.partial(_inproj_kernel, row_chunk=128),
        grid=(n_tok // tm, n_out // tn),
        in_specs=[pl.BlockSpec((tm, d), lambda i, j: (i, 0)),
                  pl.BlockSpec((1, d), lambda i, j: (0, 0)),
                  pl.BlockSpec((d, tn), lambda i, j: (0, j))],
        out_specs=pl.BlockSpec((tm, tn), lambda i, j: (i, j)),
        out_shape=jax.ShapeDtypeStruct((n_tok, n_out), BF16),
        scratch_shapes=[pltpu.VMEM((tm, d), BF16)],
        compiler_params=pltpu.CompilerParams(
            dimension_semantics=("arbitrary", "arbitrary"), vmem_limit_bytes=VMEM_LIMIT),
        name="in_projection",
    )(x2d, gain.reshape(1, d), w_bf16)


def _rope_tables(pos_ref, inv_freq_row):
    n_freq = ATTN_HEAD_DIM // 2
    n_slab = LANES // n_freq
    slab = pos_ref.shape[0] // n_slab
    assert slab * n_slab == pos_ref.shape[0]
    group = lax.broadcasted_iota(jnp.int32, (1, LANES), 1) // n_freq
    packed = jnp.zeros((slab, LANES), F32)
    for i in range(n_slab):
        packed = jnp.where(group == i, pos_ref[i * slab:(i + 1) * slab, :].astype(F32), packed)
    ang = packed * inv_freq_row
    cos_p, sin_p = jnp.cos(ang), jnp.sin(ang)

    def spread(t, i):
        t = jnp.where(group == i, t, 0.0)
        t = t + pltpu.roll(t, n_freq, axis=1)
        return t + pltpu.roll(t, 2 * n_freq, axis=1)

    assert n_slab == 4
    cos = jnp.concatenate([spread(cos_p, i) for i in range(n_slab)], axis=0)
    sin = jnp.concatenate([spread(sin_p, i) for i in range(n_slab)], axis=0)
    lane = lax.broadcasted_iota(jnp.int32, (1, LANES), 1)
    first = (lane % ATTN_HEAD_DIM) < n_freq
    return cos, jnp.where(first, -sin, 0.0), jnp.where(first, 0.0, sin)


def _rope(x, cos, sin_first, sin_second):
    half = ATTN_HEAD_DIM // 2
    from_right = pltpu.roll(x, LANES - half, axis=1)
    from_left = pltpu.roll(x, half, axis=1)
    return x * cos + from_right * sin_first + from_left * sin_second


def _attn_kernel(sink_ref, invf_ref, pos_ref, *refs, tq):
    nq = ATTN_WIDTH // _COL_BLOCK
    q_ref, g_ref = _Cols(refs[:nq]), _Cols(refs[nq:2 * nq])
    k_ref, v_ref, o_ref, kpad_ref, vpad_ref = refs[2 * nq:]
    blk = WINDOW
    nsub = tq // blk
    first_block = pl.program_id(1) == 0

    @pl.when(first_block)
    def _():
        kpad_ref[:, :blk, :] = jnp.zeros((2 * ATTN_KV_HEADS, blk, LANES), BF16)
        vpad_ref[:, :blk, :] = jnp.zeros((2 * ATTN_KV_HEADS, blk, LANES), BF16)

    @pl.when(jnp.logical_not(first_block))
    def _():
        kpad_ref[:, :blk, :] = kpad_ref[:, tq:, :]
        vpad_ref[:, :blk, :] = vpad_ref[:, tq:, :]

    cos, sin_first, sin_second = _rope_tables(pos_ref, invf_ref[...])
    lane = lax.broadcasted_iota(jnp.int32, (1, LANES), 1)
    lo_lane = lane < ATTN_HEAD_DIM

    for s in range(KV_WIDTH // LANES):
        cols = slice(s * LANES, (s + 1) * LANES)
        kk = _rope(k_ref[:, cols].astype(F32), cos, sin_first, sin_second)
        vv = v_ref[:, cols].astype(F32)
        kk_sw = pltpu.roll(kk, ATTN_HEAD_DIM, axis=1)
        vv_sw = pltpu.roll(vv, ATTN_HEAD_DIM, axis=1)
        for ref, own, swapped in ((kpad_ref, kk, kk_sw), (vpad_ref, vv, vv_sw)):
            ref[4 * s + 0, blk:, :] = jnp.where(lo_lane, own, 0.0).astype(BF16)
            ref[4 * s + 1, blk:, :] = jnp.where(lo_lane, 0.0, swapped).astype(BF16)
            ref[4 * s + 2, blk:, :] = jnp.where(lo_lane, swapped, 0.0).astype(BF16)
            ref[4 * s + 3, blk:, :] = jnp.where(lo_lane, 0.0, own).astype(BF16)

    r_idx = lax.broadcasted_iota(jnp.int32, (blk, 2 * blk), 0)
    c_idx = lax.broadcasted_iota(jnp.int32, (blk, 2 * blk), 1)
    band = (c_idx > r_idx) & (c_idx <= r_idx + blk)
    scale = (ATTN_HEAD_DIM ** -0.5) * LOG2_E
    nt = (((1,), (1,)), ((), ()))

    group_rows = ATTN_GROUP // 2 * blk
    lo_rows = lax.broadcasted_iota(jnp.int32, (group_rows, 1), 0) < blk
    band2 = jnp.concatenate([band] * (ATTN_GROUP // 2), axis=0)
    c_idx2 = jnp.concatenate([c_idx] * (ATTN_GROUP // 2), axis=0)
    for sub in range(nsub):
        rows = slice(sub * blk, (sub + 1) * blk)
        keys = slice(sub * blk, sub * blk + 2 * blk)
        if sub == 0:
            mask = band2 & (c_idx2 >= jnp.where(first_block, blk, 0))
        else:
            mask = band2
        for g in range(ATTN_KV_HEADS):
            pairs = (2 * g, 2 * g + 1)
            q2 = jnp.concatenate(
                [_rope(q_ref[rows, p * LANES:(p + 1) * LANES].astype(F32), cos[rows], sin_first[rows],
                       sin_second[rows]) for p in pairs], axis=0)
            q2 = (q2 * scale).astype(BF16)
            acc = jnp.zeros((group_rows, LANES), F32)
            inv = []
            for which in range(2):
                s = lax.dot_general(q2, kpad_ref[2 * g + which, keys, :], nt, preferred_element_type=F32)
                s = jnp.where(mask, s, -jnp.inf)
                sink = jnp.where(lo_rows, sink_ref[2 * pairs[0] + which], sink_ref[2 * pairs[1] + which]) * LOG2_E
                m = jnp.maximum(jnp.max(s, axis=-1, keepdims=True), sink)
                p = jnp.exp2(s - m)
                denom = jnp.sum(p, axis=-1, keepdims=True) + jnp.exp2(sink - m)
                inv.append(1.0 / denom)
                acc = acc + jnp.dot(p.astype(BF16), vpad_ref[2 * g + which, keys, :], preferred_element_type=F32)
            attn = acc * jnp.where(lo_lane, inv[0], inv[1])
            for i, p in enumerate(pairs):
                cols = slice(p * LANES, (p + 1) * LANES)
                gate = g_ref[rows, cols].astype(F32)
                o_ref[rows, cols] = (attn[i * blk:(i + 1) * blk] * (gate / (1.0 + jnp.exp(-gate)))).astype(o_ref.dtype)


def _attention(proj3, pos3, sinks, inv_freq_row, *, tq=512):
    b, t, _ = proj3.shape
    assert t % tq == 0 and tq % WINDOW == 0
    seg = lambda name: _segment_specs(name, (None, tq), lambda bi, i: (bi, i))
    in_specs = [pl.BlockSpec(memory_space=pltpu.SMEM),
                pl.BlockSpec((1, LANES), lambda bi, i: (0, 0)),
                pl.BlockSpec((None, tq, 1), lambda bi, i: (bi, i, 0)),
                *seg("aq"), *seg("ag"), *seg("ak"), *seg("av")]
    return pl.pallas_call(
        functools.partial(_attn_kernel, tq=tq),
        grid=(b, t // tq),
        in_specs=in_specs,
        out_specs=pl.BlockSpec((None, tq, ATTN_WIDTH), lambda bi, i: (bi, i, 0)),
        out_shape=jax.ShapeDtypeStruct((b, t, ATTN_WIDTH), BF16),
        scratch_shapes=[pltpu.VMEM((2 * ATTN_KV_HEADS, WINDOW + tq, LANES), BF16),
                        pltpu.VMEM((2 * ATTN_KV_HEADS, WINDOW + tq, LANES), BF16)],
        compiler_params=pltpu.CompilerParams(
            dimension_semantics=("arbitrary", "arbitrary"), vmem_limit_bytes=VMEM_LIMIT),
        name="swa_attention",
    )(sinks, inv_freq_row, pos3, *([proj3] * (len(in_specs) - 3)))


def _anchor_rows(b, m):
    c, w = b.shape
    g = c // m
    b3 = b.reshape(g, m, w)
    return jnp.broadcast_to(b3[:, m // 2 - 1:m // 2, :], (g, m, w)).reshape(c, w)


def _hgrn_kernel(lbp_ref, gain_ref, tril_ref, *refs, layer, chunk, n_chunks):
    ns = HGRN_WIDTH // _COL_BLOCK
    hq_ref, hf_ref, hi_ref, hg_ref = (_Cols(refs[i * ns:(i + 1) * ns]) for i in range(4))
    o_ref, st_ref, kt_ref = refs[4 * ns:]
    c = chunk
    dk, dv = HGRN_KEY_DIM, HGRN_VALUE_DIM
    w = HGRN_HEADS * dk

    @pl.when(pl.program_id(1) == 0)
    def _():
        st_ref[...] = jnp.zeros_like(st_ref)

    lbp = lbp_ref[...].astype(F32)
    e = jnp.exp(lbp - jnp.max(lbp, axis=0, keepdims=True))
    lb = jnp.sum(e[:layer + 1], axis=0, keepdims=True) / jnp.sum(e, axis=0, keepdims=True)

    for ci in range(n_chunks):
        rows = slice(ci * c, (ci + 1) * c)
        k = (1.0 - lb) / (1.0 + jnp.exp(hf_ref[rows, :].astype(F32)))
        log2_f = jnp.log(1.0 - k) * LOG2_E
        k16 = k.astype(BF16)
        hq = hq_ref[rows, :].astype(F32)
        q16 = ((hq / (1.0 + jnp.exp(-hq))) * (dk ** -0.5)).astype(BF16)
        v = hi_ref[rows, :]

        f_hi = log2_f.astype(BF16)
        f_lo = (log2_f - f_hi.astype(F32)).astype(BF16)
        cs = jnp.dot(tril_ref[...], jnp.concatenate([f_hi, f_lo], axis=1), preferred_element_type=F32)
        b = cs[:, :w] + cs[:, w:]
        b_last = b[c - 1:c, :]

        ti = lax.broadcasted_iota(jnp.int32, (c, c), 0)
        si = lax.broadcasted_iota(jnp.int32, (c, c), 1)
        txs = ti ^ si
        causal = ti > si
        nt = (((1,), (1,)), ((), ()))
        tn = (((0,), (0,)), ((), ()))

        a_mats = [None] * HGRN_HEADS
        kt_ref[...] = k16.T
        k16_t = kt_ref[...]
        for h in range(HGRN_HEADS):
            cols = slice(h * dk, (h + 1) * dk)
            gm = jnp.dot(q16[:, cols], k16_t[cols, :], preferred_element_type=F32)
            a_mats[h] = jnp.where(txs == 0, gm, 0.0)
        f = 1.0 - k
        row = lax.broadcasted_iota(jnp.int32, (c, 1), 0)
        m = c
        while m >= 2:
            half = m // 2
            if m > 4:
                z = jnp.exp2(-jnp.abs(b - _anchor_rows(b, m)))
            elif m == 4:
                u = row % 4
                z = jnp.where(u == 0, pltpu.roll(f, c - 1, axis=0),
                              jnp.where(u == 1, 1.0, jnp.where(u == 2, f, f * pltpu.roll(f, 1, axis=0))))
            else:
                z = jnp.where(row % 2 == 1, f, 1.0)
            z = z.astype(BF16)
            qz = q16 * z
            kt_ref[...] = (k16 * z).T
            kz_t = kt_ref[...]
            sel = ((txs // half) == 1) & causal
            for h in range(HGRN_HEADS):
                cols = slice(h * dk, (h + 1) * dk)
                gm = jnp.dot(qz[:, cols], kz_t[cols, :], preferred_element_type=F32)
                a_mats[h] = jnp.where(sel, gm, a_mats[h])
            m = half

        q_dec = q16 * jnp.exp2(b).astype(BF16)
        k_dec = k16 * jnp.exp2(b_last - b).astype(BF16)
        decay_last = jnp.exp2(b_last)
        gain = gain_ref[...].astype(F32)
        hg = hg_ref[rows, :].astype(F32)
        gate = hg / (1.0 + jnp.exp(-hg))
        for h in range(HGRN_HEADS):
            kc = slice(h * dk, (h + 1) * dk)
            vc = slice(h * dv, (h + 1) * dv)
            st = st_ref[h]
            o = jnp.dot(a_mats[h].astype(BF16), v[:, vc], preferred_element_type=F32)
            o = o + lax.dot_general(q_dec[:, kc], st.astype(BF16), nt, preferred_element_type=F32)
            st_ref[h] = st * decay_last[:, kc] + lax.dot_general(v[:, vc], k_dec[:, kc], tn,
                                                                  preferred_element_type=F32)
            ms = jnp.mean(o * o, axis=-1, keepdims=True)
            y = (o * lax.rsqrt(ms + NORM_EPS)) * gain[:, vc]
            o_ref[rows, vc] = (y * gate[:, vc]).astype(o_ref.dtype)


def _hgrn(proj3, lb_params, gain_row, layer, *, chunk=128, n_chunks=4):
    b, t, _ = proj3.shape
    rows = chunk * n_chunks
    assert t % rows == 0 and chunk & (chunk - 1) == 0
    w = HGRN_WIDTH
    tril = jnp.asarray(np.tril(np.ones((chunk, chunk), np.float32)), dtype=BF16)
    seg = lambda name: _segment_specs(name, (None, rows), lambda bi, i: (bi, i))
    in_specs = [pl.BlockSpec(lb_params.shape, lambda bi, i: (0, 0)),
                pl.BlockSpec((1, w), lambda bi, i: (0, 0)),
                pl.BlockSpec((chunk, chunk), lambda bi, i: (0, 0)),
                *seg("hq"), *seg("hf"), *seg("hi"), *seg("hg")]
    return pl.pallas_call(
        functools.partial(_hgrn_kernel, layer=layer, chunk=chunk, n_chunks=n_chunks),
        grid=(b, t // rows),
        in_specs=in_specs,
        out_specs=pl.BlockSpec((None, rows, w), lambda bi, i: (bi, i, 0)),
        out_shape=jax.ShapeDtypeStruct((b, t, w), BF16),
        scratch_shapes=[pltpu.VMEM((HGRN_HEADS, HGRN_VALUE_DIM, HGRN_KEY_DIM), F32),
                        pltpu.VMEM((w, chunk), BF16)],
        compiler_params=pltpu.CompilerParams(
            dimension_semantics=("arbitrary", "arbitrary"), vmem_limit_bytes=VMEM_LIMIT),
        name="hgrn2",
    )(lb_params, gain_row, tril, *([proj3] * (len(in_specs) - 3)))


def _sigmoid(x):
    return 1.0 / (1.0 + jnp.exp(-x))


def _merge_kernel(ga_ref, gh_ref, *refs, final_norm):
    ng = (len(refs) - 6) // 2
    ma_ref, mh_ref = _Cols(refs[:ng]), _Cols(refs[ng:2 * ng])
    x_ref, wa_ref, wh_ref, wo_ref, fg_ref, o_ref = refs[2 * ng:]
    tm = o_ref.shape[0]
    halves = [slice(0, tm // 2), slice(tm // 2, tm)]

    def branches(r):
        ya = jnp.dot(ga_ref[r, :], wa_ref[...], preferred_element_type=F32)
        yh = jnp.dot(gh_ref[r, :], wh_ref[...], preferred_element_type=F32)
        return ya, yh

    def merge(r, ya, yh):
        return (_sigmoid(ma_ref[r, :].astype(F32)) * ya + _sigmoid(mh_ref[r, :].astype(F32)) * yh).astype(BF16)

    def project(r, merged):
        return x_ref[r, :] + jnp.dot(merged, wo_ref[...], preferred_element_type=F32)

    def finish(r, y):
        if final_norm:
            ms = jnp.mean(y * y, axis=-1, keepdims=True)
            y = (y * lax.rsqrt(ms + NORM_EPS)) * fg_ref[...]
        o_ref[r, :] = y

    b0 = branches(halves[0])
    b1 = branches(halves[1])
    m0 = merge(halves[0], *b0)
    y0 = project(halves[0], m0)
    m1 = merge(halves[1], *b1)
    finish(halves[0], y0)
    y1 = project(halves[1], m1)
    finish(halves[1], y1)


def _merge(ga, gh, proj2, x2d, wa, wh, wo, fgain, final_norm, *, tm=512):
    n_tok, d = x2d.shape
    assert n_tok % tm == 0
    const = lambda shape: pl.BlockSpec(shape, lambda i: (0, 0), pipeline_mode=pl.Buffered(1))
    seg = lambda name: _segment_specs(name, (tm,), lambda i: (i,))
    gate_specs = [*seg("ma"), *seg("mh")]
    return pl.pallas_call(
        functools.partial(_merge_kernel, final_norm=final_norm),
        grid=(n_tok // tm,),
        in_specs=[pl.BlockSpec((tm, ATTN_WIDTH), lambda i: (i, 0)),
                  pl.BlockSpec((tm, HGRN_WIDTH), lambda i: (i, 0)),
                  *gate_specs,
                  pl.BlockSpec((tm, d), lambda i: (i, 0)),
                  const(wa.shape), const(wh.shape), const(wo.shape), const((1, d))],
        out_specs=pl.BlockSpec((tm, d), lambda i: (i, 0)),
        out_shape=jax.ShapeDtypeStruct((n_tok, d), F32),
        compiler_params=pltpu.CompilerParams(
            dimension_semantics=("arbitrary",), vmem_limit_bytes=VMEM_LIMIT),
        name="merge_out",
    )(ga, gh, *([proj2] * len(gate_specs)), x2d, wa, wh, wo, fgain.reshape(1, d))


def kernel(x, positions, norm_gain, w_in, attn_sinks, hgrn_lower_bounds, hgrn_norm_gain, w_attn_out,
           w_hgrn_out, w_o, final_norm_gain):
    b, t, d = x.shape
    depth = w_in.shape[0]
    assert w_in.shape[2] == IN_WIDTH and _SEG_OFF["ma"][1] == d
    half = ATTN_HEAD_DIM // 2
    inv_freq = ROPE_THETA ** (-jnp.arange(half, dtype=F32) / half)
    inv_freq_row = jnp.tile(inv_freq, LANES // half).reshape(1, LANES)
    pos3 = positions.astype(jnp.int32).reshape(b, t, 1)

    x2d = x.reshape(b * t, d)
    for l in range(depth):
        proj = _in_projection(x2d, norm_gain[l], w_in[l].astype(BF16))
        proj3 = proj.reshape(b, t, IN_WIDTH)
        ga = _attention(proj3, pos3, attn_sinks[l].astype(F32), inv_freq_row)
        gh = _hgrn(proj3, hgrn_lower_bounds.astype(F32), hgrn_norm_gain[l].reshape(1, HGRN_WIDTH).astype(F32), l)
        x2d = _merge(ga.reshape(b * t, ATTN_WIDTH), gh.reshape(b * t, HGRN_WIDTH), proj, x2d,
                     w_attn_out[l].astype(BF16), w_hgrn_out[l].astype(BF16), w_o[l].astype(BF16),
                     final_norm_gain, l == depth - 1)
    return x2d.reshape(b, t, d)
```

```python
import functools

import numpy as np
import jax
import jax.numpy as jnp
from jax import lax
from jax.experimental import pallas as pl
from jax.experimental.pallas import tpu as pltpu

F32 = jnp.float32
BF16 = jnp.bfloat16

ATTN_HEADS = 16
ATTN_KV_HEADS = 4
ATTN_HEAD_DIM = 64
ATTN_GROUP = ATTN_HEADS // ATTN_KV_HEADS
ATTN_WIDTH = ATTN_HEADS * ATTN_HEAD_DIM
KV_WIDTH = ATTN_KV_HEADS * ATTN_HEAD_DIM
WINDOW = 128
ROPE_THETA = 10000.0
HGRN_HEADS = 8
HGRN_KEY_DIM = 128
HGRN_VALUE_DIM = 128
HGRN_WIDTH = HGRN_HEADS * HGRN_VALUE_DIM
NORM_EPS = 1e-6
LOG2_E = 1.4426950408889634

LANES = 128
VMEM_LIMIT = 56 * 1024 * 1024

_SEGMENTS = (("aq", 1024), ("ak", 256), ("av", 256), ("ag", 1024), ("hq", 1024),
             ("hf", 1024), ("hi", 1024), ("hg", 1024), ("ma", 2048), ("mh", 2048))
_SEG_OFF = {}
IN_WIDTH = 0
for _name, _w in _SEGMENTS:
    _SEG_OFF[_name] = (IN_WIDTH, _w)
    IN_WIDTH += _w
_COL_BLOCK = 512


def _segment_specs(name, lead_block, index_fn):
    off, width = _SEG_OFF[name]
    cb = min(_COL_BLOCK, width)
    assert off % cb == 0 and width % cb == 0
    return [pl.BlockSpec(lead_block + (cb,), lambda *ids, c=(off + i * cb) // cb: index_fn(*ids) + (c,))
            for i in range(width // cb)]


class _Cols:
    def __init__(self, refs):
        self._refs = refs
        self._cb = refs[0].shape[-1]

    def __getitem__(self, idx):
        rows, cols = idx
        start = 0 if cols.start is None else cols.start
        stop = self._cb * len(self._refs) if cols.stop is None else cols.stop
        pieces, c = [], start
        while c < stop:
            i = c // self._cb
            hi = min(stop, (i + 1) * self._cb)
            pieces.append(self._refs[i][rows, c - i * self._cb:hi - i * self._cb])
            c = hi
        return pieces[0] if len(pieces) == 1 else jnp.concatenate(pieces, axis=1)


def _sigmoid(x):
    return 1.0 / (1.0 + jnp.exp(-x))


def _inproj_kernel(x_ref, g_ref, w_ref, o_ref, h_ref, *, row_chunk):
    @pl.when(pl.program_id(1) == 0)
    def _():
        def body(r, carry):
            rows = pl.ds(pl.multiple_of(r * row_chunk, row_chunk), row_chunk)
            x = x_ref[rows, :]
            ms = jnp.mean(x * x, axis=-1, keepdims=True)
            h_ref[rows, :] = ((x * lax.rsqrt(ms + NORM_EPS)) * g_ref[...]).astype(BF16)
            return carry
        lax.fori_loop(0, x_ref.shape[0] // row_chunk, body, 0)

    o_ref[...] = jnp.dot(h_ref[...], w_ref[...], preferred_element_type=F32).astype(o_ref.dtype)


def _in_projection(x2d, gain, w_bf16, *, tm=1024, tn=1792):
    n_tok, d = x2d.shape
    n_out = w_bf16.shape[1]
    assert n_tok % tm == 0 and n_out % tn == 0
    return pl.pallas_call(
        functools.partial(_inproj_kernel, row_chunk=128),
        grid=(n_tok // tm, n_out // tn),
        in_specs=[pl.BlockSpec((tm, d), lambda i, j: (i, 0)),
                  pl.BlockSpec((1, d), lambda i, j: (0, 0)),
                  pl.BlockSpec((d, tn), lambda i, j: (0, j))],
        out_specs=pl.BlockSpec((tm, tn), lambda i, j: (i, j)),
        out_shape=jax.ShapeDtypeStruct((n_tok, n_out), BF16),
        scratch_shapes=[pltpu.VMEM((tm, d), BF16)],
        compiler_params=pltpu.CompilerParams(
            dimension_semantics=("arbitrary", "arbitrary"), vmem_limit_bytes=VMEM_LIMIT),
        name="in_projection",
    )(x2d, gain.reshape(1, d), w_bf16)


def _anchor_rows(b, m):
    c, w = b.shape
    g = c // m
    b3 = b.reshape(g, m, w)
    return jnp.broadcast_to(b3[:, m // 2 - 1:m // 2, :], (g, m, w)).reshape(c, w)


def _hgrn_kernel(lbp_ref, gain_ref, tril_ref, *refs, layer, chunk, n_chunks):
    ns = HGRN_WIDTH // _COL_BLOCK
    hq_ref, hf_ref, hi_ref, hg_ref = (_Cols(refs[i * ns:(i + 1) * ns]) for i in range(4))
    o_ref, st_ref = refs[4 * ns:]
    c = chunk
    dk, dv = HGRN_KEY_DIM, HGRN_VALUE_DIM
    w = HGRN_HEADS * dk

    @pl.when(pl.program_id(1) == 0)
    def _():
        st_ref[...] = jnp.zeros_like(st_ref)

    lbp = lbp_ref[...].astype(F32)
    e = jnp.exp(lbp - jnp.max(lbp, axis=0, keepdims=True))
    lb = jnp.sum(e[:layer + 1], axis=0, keepdims=True) / jnp.sum(e, axis=0, keepdims=True)

    for ci in range(n_chunks):
        rows = slice(ci * c, (ci + 1) * c)
        k = (1.0 - lb) / (1.0 + jnp.exp(hf_ref[rows, :].astype(F32)))
        log2_f = jnp.log(1.0 - k) * LOG2_E
        k16 = k.astype(BF16)
        hq = hq_ref[rows, :].astype(F32)
        q16 = ((hq / (1.0 + jnp.exp(-hq))) * (dk ** -0.5)).astype(BF16)
        v = hi_ref[rows, :]

        f_hi = log2_f.astype(BF16)
        f_lo = (log2_f - f_hi.astype(F32)).astype(BF16)
        cs = jnp.dot(tril_ref[...], jnp.concatenate([f_hi, f_lo], axis=1), preferred_element_type=F32)
        b = cs[:, :w] + cs[:, w:]
        b_last = b[c - 1:c, :]

        ti = lax.broadcasted_iota(jnp.int32, (c, c), 0)
        si = lax.broadcasted_iota(jnp.int32, (c, c), 1)
        txs = ti ^ si
        causal = ti > si
        nt = (((1,), (1,)), ((), ()))
        tn = (((0,), (0,)), ((), ()))

        a_mats = [None] * HGRN_HEADS
        for h in range(HGRN_HEADS):
            cols = slice(h * dk, (h + 1) * dk)
            gm = lax.dot_general(q16[:, cols], k16[:, cols], nt, preferred_element_type=F32)
            a_mats[h] = jnp.where(txs == 0, gm, 0.0)
        f = 1.0 - k
        row = lax.broadcasted_iota(jnp.int32, (c, 1), 0)
        m = c
        while m >= 2:
            half = m // 2
            if m > 4:
                z = jnp.exp2(-jnp.abs(b - _anchor_rows(b, m)))
            elif m == 4:
                u = row % 4
                z = jnp.where(u == 0, pltpu.roll(f, c - 1, axis=0),
                              jnp.where(u == 1, 1.0, jnp.where(u == 2, f, f * pltpu.roll(f, 1, axis=0))))
            else:
                z = jnp.where(row % 2 == 1, f, 1.0)
            z = z.astype(BF16)
            qz = q16 * z
            kz = k16 * z
            sel = ((txs // half) == 1) & causal
            for h in range(HGRN_HEADS):
                cols = slice(h * dk, (h + 1) * dk)
                gm = lax.dot_general(qz[:, cols], kz[:, cols], nt, preferred_element_type=F32)
                a_mats[h] = jnp.where(sel, gm, a_mats[h])
            m = half

        q_dec = q16 * jnp.exp2(b).astype(BF16)
        k_dec = k16 * jnp.exp2(b_last - b).astype(BF16)
        decay_last = jnp.exp2(b_last)
        gain = gain_ref[...].astype(F32)
        hg = hg_ref[rows, :].astype(F32)
        gate = hg / (1.0 + jnp.exp(-hg))
        for h in range(HGRN_HEADS):
            kc = slice(h * dk, (h + 1) * dk)
            vc = slice(h * dv, (h + 1) * dv)
            st = st_ref[h]
            o = jnp.dot(a_mats[h].astype(BF16), v[:, vc], preferred_element_type=F32)
            o = o + lax.dot_general(q_dec[:, kc], st.astype(BF16), nt, preferred_element_type=F32)
            st_ref[h] = st * decay_last[:, kc] + lax.dot_general(v[:, vc], k_dec[:, kc], tn,
                                                                  preferred_element_type=F32)
            ms = jnp.mean(o * o, axis=-1, keepdims=True)
            y = (o * lax.rsqrt(ms + NORM_EPS)) * gain[:, vc]
            o_ref[rows, vc] = (y * gate[:, vc]).astype(o_ref.dtype)


def _hgrn(proj3, lb_params, gain_row, layer, *, chunk=128, n_chunks=4):
    b, t, _ = proj3.shape
    rows = chunk * n_chunks
    assert t % rows == 0 and chunk & (chunk - 1) == 0
    w = HGRN_WIDTH
    tril = jnp.asarray(np.tril(np.ones((chunk, chunk), np.float32)), dtype=BF16)
    seg = lambda name: _segment_specs(name, (None, rows), lambda bi, i: (bi, i))
    in_specs = [pl.BlockSpec(lb_params.shape, lambda bi, i: (0, 0)),
                pl.BlockSpec((1, w), lambda bi, i: (0, 0)),
                pl.BlockSpec((chunk, chunk), lambda bi, i: (0, 0)),
                *seg("hq"), *seg("hf"), *seg("hi"), *seg("hg")]
    return pl.pallas_call(
        functools.partial(_hgrn_kernel, layer=layer, chunk=chunk, n_chunks=n_chunks),
        grid=(b, t // rows),
        in_specs=in_specs,
        out_specs=pl.BlockSpec((None, rows, w), lambda bi, i: (bi, i, 0)),
        out_shape=jax.ShapeDtypeStruct((b, t, w), BF16),
        scratch_shapes=[pltpu.VMEM((HGRN_HEADS, HGRN_VALUE_DIM, HGRN_KEY_DIM), F32)],
        compiler_params=pltpu.CompilerParams(
            dimension_semantics=("arbitrary", "arbitrary"), vmem_limit_bytes=VMEM_LIMIT),
        name="hgrn2",
    )(lb_params, gain_row, tril, *([proj3] * (len(in_specs) - 3)))


def _rope_tables(pos_ref, inv_freq_row):
    n_freq = ATTN_HEAD_DIM // 2
    n_slab = LANES // n_freq
    slab = pos_ref.shape[0] // n_slab
    assert slab * n_slab == pos_ref.shape[0]
    group = lax.broadcasted_iota(jnp.int32, (1, LANES), 1) // n_freq
    packed = jnp.zeros((slab, LANES), F32)
    for i in range(n_slab):
        packed = jnp.where(group == i, pos_ref[i * slab:(i + 1) * slab, :].astype(F32), packed)
    ang = packed * inv_freq_row
    cos_p, sin_p = jnp.cos(ang), jnp.sin(ang)

    def spread(t, i):
        t = jnp.where(group == i, t, 0.0)
        t = t + pltpu.roll(t, n_freq, axis=1)
        return t + pltpu.roll(t, 2 * n_freq, axis=1)

    assert n_slab == 4
    cos = jnp.concatenate([spread(cos_p, i) for i in range(n_slab)], axis=0)
    sin = jnp.concatenate([spread(sin_p, i) for i in range(n_slab)], axis=0)
    lane = lax.broadcasted_iota(jnp.int32, (1, LANES), 1)
    first = (lane % ATTN_HEAD_DIM) < n_freq
    return cos, jnp.where(first, -sin, 0.0), jnp.where(first, 0.0, sin)


def _rope(x, cos, sin_first, sin_second):
    half = ATTN_HEAD_DIM // 2
    from_right = pltpu.roll(x, LANES - half, axis=1)
    from_left = pltpu.roll(x, half, axis=1)
    return x * cos + from_right * sin_first + from_left * sin_second


class _AttnBlock:
    def __init__(self, first_block, sink_ref, invf_ref, pos_ref, q_ref, g_ref, k_ref, v_ref, o_ref, kpad_ref,
                 vpad_ref, *, tq):
        self.first_block, self.sink_ref, self.invf_ref, self.pos_ref = first_block, sink_ref, invf_ref, pos_ref
        self.q_ref, self.g_ref, self.k_ref, self.v_ref, self.o_ref = q_ref, g_ref, k_ref, v_ref, o_ref
        self.kpad_ref, self.vpad_ref, self.tq = kpad_ref, vpad_ref, tq
        self.items = [(sub, g) for sub in range(tq // WINDOW) for g in range(ATTN_KV_HEADS)]
        self.lo_lane = lax.broadcasted_iota(jnp.int32, (1, LANES), 1) < ATTN_HEAD_DIM

    def prepare(self):
        blk, tq, kpad_ref, vpad_ref, lo_lane = WINDOW, self.tq, self.kpad_ref, self.vpad_ref, self.lo_lane

        @pl.when(self.first_block)
        def _():
            kpad_ref[:, :blk, :] = jnp.zeros((2 * ATTN_KV_HEADS, blk, LANES), BF16)
            vpad_ref[:, :blk, :] = jnp.zeros((2 * ATTN_KV_HEADS, blk, LANES), BF16)

        @pl.when(jnp.logical_not(self.first_block))
        def _():
            kpad_ref[:, :blk, :] = kpad_ref[:, tq:, :]
            vpad_ref[:, :blk, :] = vpad_ref[:, tq:, :]

        self.trig = _rope_tables(self.pos_ref, self.invf_ref[...])
        for s in range(KV_WIDTH // LANES):
            cols = slice(s * LANES, (s + 1) * LANES)
            kk = _rope(self.k_ref[:, cols].astype(F32), *self.trig)
            vv = self.v_ref[:, cols].astype(F32)
            kk_sw = pltpu.roll(kk, ATTN_HEAD_DIM, axis=1)
            vv_sw = pltpu.roll(vv, ATTN_HEAD_DIM, axis=1)
            for ref, own, swapped in ((kpad_ref, kk, kk_sw), (vpad_ref, vv, vv_sw)):
                ref[4 * s + 0, blk:, :] = jnp.where(lo_lane, own, 0.0).astype(BF16)
                ref[4 * s + 1, blk:, :] = jnp.where(lo_lane, 0.0, swapped).astype(BF16)
                ref[4 * s + 2, blk:, :] = jnp.where(lo_lane, swapped, 0.0).astype(BF16)
                ref[4 * s + 3, blk:, :] = jnp.where(lo_lane, 0.0, own).astype(BF16)

    def scores(self, item):
        sub, g = item
        blk = WINDOW
        rows = slice(sub * blk, (sub + 1) * blk)
        keys = slice(sub * blk, sub * blk + 2 * blk)
        scale = (ATTN_HEAD_DIM ** -0.5) * LOG2_E
        nt = (((1,), (1,)), ((), ()))
        q2 = jnp.concatenate(
            [_rope(self.q_ref[rows, p * LANES:(p + 1) * LANES].astype(F32), *(t[rows] for t in self.trig))
             for p in (2 * g, 2 * g + 1)], axis=0)
        q2 = (q2 * scale).astype(BF16)
        return [lax.dot_general(q2, self.kpad_ref[2 * g + which, keys, :], nt, preferred_element_type=F32)
                for which in range(2)]

    def softmax(self, item, s_pair):
        sub, g = item
        blk = WINDOW
        n_stack = ATTN_GROUP // 2
        r_idx = lax.broadcasted_iota(jnp.int32, (blk, 2 * blk), 0)
        c_idx = lax.broadcasted_iota(jnp.int32, (blk, 2 * blk), 1)
        mask = (c_idx > r_idx) & (c_idx <= r_idx + blk)
        if sub == 0:
            mask = mask & (c_idx >= jnp.where(self.first_block, blk, 0))
        mask = jnp.concatenate([mask] * n_stack, axis=0)
        lo_rows = lax.broadcasted_iota(jnp.int32, (n_stack * blk, 1), 0) < blk
        out = []
        for which, s in enumerate(s_pair):
            s = jnp.where(mask, s, -jnp.inf)
            sink = jnp.where(lo_rows, self.sink_ref[4 * g + which], self.sink_ref[4 * g + 2 + which]) * LOG2_E
            m = jnp.maximum(jnp.max(s, axis=-1, keepdims=True), sink)
            p = jnp.exp2(s - m)
            denom = jnp.sum(p, axis=-1, keepdims=True) + jnp.exp2(sink - m)
            out.append((p.astype(BF16), 1.0 / denom))
        return out

    def values(self, item, pw):
        sub, g = item
        keys = slice(sub * WINDOW, sub * WINDOW + 2 * WINDOW)
        acc = jnp.dot(pw[0][0], self.vpad_ref[2 * g, keys, :], preferred_element_type=F32)
        acc = acc + jnp.dot(pw[1][0], self.vpad_ref[2 * g + 1, keys, :], preferred_element_type=F32)
        return acc * jnp.where(self.lo_lane, pw[0][1], pw[1][1])

    def store(self, item, attn):
        sub, g = item
        blk = WINDOW
        rows = slice(sub * blk, (sub + 1) * blk)
        for i, p in enumerate((2 * g, 2 * g + 1)):
            cols = slice(p * LANES, (p + 1) * LANES)
            gate = self.g_ref[rows, cols].astype(F32)
            self.o_ref[rows, cols] = (attn[i * blk:(i + 1) * blk] * (gate / (1.0 + jnp.exp(-gate)))
                                      ).astype(self.o_ref.dtype)


def _attn_merge_kernel(sink_ref, invf_ref, pos_ref, *refs, tm, blocks_per_seq, final_norm):
    nq = ATTN_WIDTH // _COL_BLOCK
    q_ref, g_ref = _Cols(refs[:nq]), _Cols(refs[nq:2 * nq])
    k_ref, v_ref, gh_ref = refs[2 * nq:2 * nq + 3]
    rest = refs[2 * nq + 3:]
    ng = (len(rest) - 10) // 2
    ma_ref, mh_ref = _Cols(rest[:ng]), _Cols(rest[ng:2 * ng])
    x_ref, wa_ref, wh_ref, wo_ref, fg_ref, o_ref, kpad_ref, vpad_ref, ga_cur, ga_next = rest[2 * ng:]
    i = pl.program_id(0)
    first_block = lax.rem(i, blocks_per_seq) == 0

    @pl.when(i == 0)
    def _():
        ga_cur[...] = jnp.zeros_like(ga_cur)

    attn = _AttnBlock(first_block, sink_ref, invf_ref, pos_ref, q_ref, g_ref, k_ref, v_ref, ga_next, kpad_ref,
                      vpad_ref, tq=tm)
    items = attn.items
    half = len(items) // 2
    attn.prepare()
    scores = [attn.scores(it) for it in items]
    ya = jnp.dot(ga_cur[...], wa_ref[...], preferred_element_type=F32)
    probs = [attn.softmax(it, s) for it, s in zip(items[:half], scores[:half])]
    yh = jnp.dot(gh_ref[...], wh_ref[...], preferred_element_type=F32)
    probs += [attn.softmax(it, s) for it, s in zip(items[half:], scores[half:])]
    outs = [attn.values(it, pw) for it, pw in zip(items, probs)]
    merged = (_sigmoid(ma_ref[:, :].astype(F32)) * ya + _sigmoid(mh_ref[:, :].astype(F32)) * yh).astype(BF16)
    y = x_ref[...] + jnp.dot(merged, wo_ref[...], preferred_element_type=F32)
    for it, o in zip(items, outs):
        attn.store(it, o)
    if final_norm:
        ms = jnp.mean(y * y, axis=-1, keepdims=True)
        y = (y * lax.rsqrt(ms + NORM_EPS)) * fg_ref[...]
    o_ref[...] = y
    ga_cur[...] = ga_next[...]


def _attn_merge(proj2, pos2, sinks, inv_freq_row, gh, x2d, wa, wh, wo, fgain, final_norm, *, seq_len, tm=256):
    n_tok, d = x2d.shape
    assert n_tok % tm == 0 and seq_len % tm == 0 and tm % WINDOW == 0
    nb = n_tok // tm
    cur = lambda i: (jnp.minimum(i, nb - 1),)
    prv = lambda i: (jnp.maximum(i - 1, 0),)
    const = lambda shape: pl.BlockSpec(shape, lambda i: (0, 0), pipeline_mode=pl.Buffered(1))
    seg_c = lambda name: _segment_specs(name, (tm,), cur)
    seg_p = lambda name: _segment_specs(name, (tm,), prv)
    attn_specs = [*seg_c("aq"), *seg_c("ag"), *seg_c("ak"), *seg_c("av")]
    gate_specs = [*seg_p("ma"), *seg_p("mh")]
    in_specs = [pl.BlockSpec(memory_space=pltpu.SMEM),
                pl.BlockSpec((1, LANES), lambda i: (0, 0)),
                pl.BlockSpec((tm, 1), lambda i: cur(i) + (0,)),
                *attn_specs,
                pl.BlockSpec((tm, HGRN_WIDTH), lambda i: prv(i) + (0,)),
                *gate_specs,
                pl.BlockSpec((tm, d), lambda i: prv(i) + (0,)),
                const(wa.shape), const(wh.shape), const(wo.shape), const((1, d))]
    return pl.pallas_call(
        functools.partial(_attn_merge_kernel, tm=tm, blocks_per_seq=seq_len // tm, final_norm=final_norm),
        grid=(nb + 1,),
        in_specs=in_specs,
        out_specs=pl.BlockSpec((tm, d), lambda i: prv(i) + (0,)),
        out_shape=jax.ShapeDtypeStruct((n_tok, d), F32),
        scratch_shapes=[pltpu.VMEM((2 * ATTN_KV_HEADS, WINDOW + tm, LANES), BF16),
                        pltpu.VMEM((2 * ATTN_KV_HEADS, WINDOW + tm, LANES), BF16),
                        pltpu.VMEM((tm, ATTN_WIDTH), BF16),
                        pltpu.VMEM((tm, ATTN_WIDTH), BF16)],
        compiler_params=pltpu.CompilerParams(
            dimension_semantics=("arbitrary",), vmem_limit_bytes=VMEM_LIMIT),
        name="attention_merge",
    )(sinks, inv_freq_row, pos2, *([proj2] * len(attn_specs)), gh, *([proj2] * len(gate_specs)), x2d, wa, wh, wo,
      fgain.reshape(1, d))


def kernel(x, positions, norm_gain, w_in, attn_sinks, hgrn_lower_bounds, hgrn_norm_gain, w_attn_out,
           w_hgrn_out, w_o, final_norm_gain):
    b, t, d = x.shape
    depth = w_in.shape[0]
    assert w_in.shape[2] == IN_WIDTH and _SEG_OFF["ma"][1] == d
    half = ATTN_HEAD_DIM // 2
    inv_freq = ROPE_THETA ** (-jnp.arange(half, dtype=F32) / half)
    inv_freq_row = jnp.tile(inv_freq, LANES // half).reshape(1, LANES)
    pos2 = positions.astype(jnp.int32).reshape(b * t, 1)

    x2d = x.reshape(b * t, d)
    for l in range(depth):
        proj = _in_projection(x2d, norm_gain[l], w_in[l].astype(BF16))
        gh = _hgrn(proj.reshape(b, t, IN_WIDTH), hgrn_lower_bounds.astype(F32),
                   hgrn_norm_gain[l].reshape(1, HGRN_WIDTH).astype(F32), l)
        x2d = _attn_merge(proj, pos2, attn_sinks[l].astype(F32), inv_freq_row, gh.reshape(b * t, HGRN_WIDTH), x2d,
                          w_attn_out[l].astype(BF16), w_hgrn_out[l].astype(BF16), w_o[l].astype(BF16),
                          final_norm_gain, l == depth - 1, seq_len=t)
    return x2d.reshape(b, t, d)
```

```python
import functools

import numpy as np
import jax
import jax.numpy as jnp
from jax import lax
from jax.experimental import pallas as pl
from jax.experimental.pallas import tpu as pltpu

F32 = jnp.float32
BF16 = jnp.bfloat16

ATTN_HEADS = 16
ATTN_KV_HEADS = 4
ATTN_HEAD_DIM = 64
ATTN_GROUP = ATTN_HEADS // ATTN_KV_HEADS
ATTN_WIDTH = ATTN_HEADS * ATTN_HEAD_DIM
KV_WIDTH = ATTN_KV_HEADS * ATTN_HEAD_DIM
WINDOW = 128
ROPE_THETA = 10000.0
HGRN_HEADS = 8
HGRN_KEY_DIM = 128
HGRN_VALUE_DIM = 128
HGRN_WIDTH = HGRN_HEADS * HGRN_VALUE_DIM
NORM_EPS = 1e-6
LOG2_E = 1.4426950408889634

LANES = 128
VMEM_LIMIT = 56 * 1024 * 1024

_SEGMENTS = (("aq", 1024), ("ak", 256), ("av", 256), ("ag", 1024), ("hq", 1024),
             ("hf", 1024), ("hi", 1024), ("hg", 1024), ("ma", 2048), ("mh", 2048))
_SEG_OFF = {}
IN_WIDTH = 0
for _name, _w in _SEGMENTS:
    _SEG_OFF[_name] = (IN_WIDTH, _w)
    IN_WIDTH += _w
_COL_BLOCK = 512


def _segment_specs(name, lead_block, index_fn):
    off, width = _SEG_OFF[name]
    cb = min(_COL_BLOCK, width)
    assert off % cb == 0 and width % cb == 0
    return [pl.BlockSpec(lead_block + (cb,), lambda *ids, c=(off + i * cb) // cb: index_fn(*ids) + (c,))
            for i in range(width // cb)]


class _Cols:
    def __init__(self, refs):
        self._refs = refs
        self._cb = refs[0].shape[-1]

    def __getitem__(self, idx):
        rows, cols = idx
        start = 0 if cols.start is None else cols.start
        stop = self._cb * len(self._refs) if cols.stop is None else cols.stop
        pieces, c = [], start
        while c < stop:
            i = c // self._cb
            hi = min(stop, (i + 1) * self._cb)
            pieces.append(self._refs[i][rows, c - i * self._cb:hi - i * self._cb])
            c = hi
        return pieces[0] if len(pieces) == 1 else jnp.concatenate(pieces, axis=1)


def _sigmoid(x):
    return 1.0 / (1.0 + jnp.exp(-x))


def _inproj_kernel(x_ref, g_ref, w_ref, o_ref, h_ref, *, row_chunk):
    @pl.when(pl.program_id(1) == 0)
    def _():
        def body(r, carry):
            rows = pl.ds(pl.multiple_of(r * row_chunk, row_chunk), row_chunk)
            x = x_ref[rows, :]
            ms = jnp.mean(x * x, axis=-1, keepdims=True)
            h_ref[rows, :] = ((x * lax.rsqrt(ms + NORM_EPS)) * g_ref[...]).astype(BF16)
            return carry
        lax.fori_loop(0, x_ref.shape[0] // row_chunk, body, 0)

    o_ref[...] = jnp.dot(h_ref[...], w_ref[...], preferred_element_type=F32).astype(o_ref.dtype)


def _in_projection(x2d, gain, w_bf16, *, tm=1024, tn=1792):
    n_tok, d = x2d.shape
    n_out = w_bf16.shape[1]
    assert n_tok % tm == 0 and n_out % tn == 0
    return pl.pallas_call(
        functools.partial(_inproj_kernel, row_chunk=128),
        grid=(n_tok // tm, n_out // tn),
        in_specs=[pl.BlockSpec((tm, d), lambda i, j: (i, 0)),
                  pl.BlockSpec((1, d), lambda i, j: (0, 0)),
                  pl.BlockSpec((d, tn), lambda i, j: (0, j))],
        out_specs=pl.BlockSpec((tm, tn), lambda i, j: (i, j)),
        out_shape=jax.ShapeDtypeStruct((n_tok, n_out), BF16),
        scratch_shapes=[pltpu.VMEM((tm, d), BF16)],
        compiler_params=pltpu.CompilerParams(
            dimension_semantics=("arbitrary", "arbitrary"), vmem_limit_bytes=VMEM_LIMIT),
        name="in_projection",
    )(x2d, gain.reshape(1, d), w_bf16)


def _anchor_rows(b, m):
    c, w = b.shape
    g = c // m
    b3 = b.reshape(g, m, w)
    return jnp.broadcast_to(b3[:, m // 2 - 1:m // 2, :], (g, m, w)).reshape(c, w)


def _hgrn_kernel(lbp_ref, gain_ref, tril_ref, *refs, layer, chunk, n_chunks):
    ns = HGRN_WIDTH // _COL_BLOCK
    hq_ref, hf_ref, hi_ref, hg_ref = (_Cols(refs[i * ns:(i + 1) * ns]) for i in range(4))
    o_ref, st_ref = refs[4 * ns:]
    c = chunk
    dk, dv = HGRN_KEY_DIM, HGRN_VALUE_DIM
    w = HGRN_HEADS * dk

    @pl.when(pl.program_id(1) == 0)
    def _():
        st_ref[...] = jnp.zeros_like(st_ref)

    lbp = lbp_ref[...].astype(F32)
    e = jnp.exp(lbp - jnp.max(lbp, axis=0, keepdims=True))
    lb = jnp.sum(e[:layer + 1], axis=0, keepdims=True) / jnp.sum(e, axis=0, keepdims=True)

    for ci in range(n_chunks):
        rows = slice(ci * c, (ci + 1) * c)
        k = (1.0 - lb) / (1.0 + jnp.exp(hf_ref[rows, :].astype(F32)))
        log2_f = jnp.log(1.0 - k) * LOG2_E
        k16 = k.astype(BF16)
        hq = hq_ref[rows, :].astype(F32)
        q16 = ((hq / (1.0 + jnp.exp(-hq))) * (dk ** -0.5)).astype(BF16)
        v = hi_ref[rows, :]

        f_hi = log2_f.astype(BF16)
        f_lo = (log2_f - f_hi.astype(F32)).astype(BF16)
        cs = jnp.dot(tril_ref[...], jnp.concatenate([f_hi, f_lo], axis=1), preferred_element_type=F32)
        b = cs[:, :w] + cs[:, w:]
        b_last = b[c - 1:c, :]

        ti = lax.broadcasted_iota(jnp.int32, (c, c), 0)
        si = lax.broadcasted_iota(jnp.int32, (c, c), 1)
        txs = ti ^ si
        causal = ti > si
        nt = (((1,), (1,)), ((), ()))
        tn = (((0,), (0,)), ((), ()))

        a_mats = [None] * HGRN_HEADS
        for h in range(HGRN_HEADS):
            cols = slice(h * dk, (h + 1) * dk)
            gm = lax.dot_general(q16[:, cols], k16[:, cols], nt, preferred_element_type=F32)
            a_mats[h] = jnp.where(txs == 0, gm, 0.0)
        f = 1.0 - k
        row = lax.broadcasted_iota(jnp.int32, (c, 1), 0)
        m = c
        while m >= 2:
            half = m // 2
            if m > 4:
                z = jnp.exp2(-jnp.abs(b - _anchor_rows(b, m)))
            elif m == 4:
                u = row % 4
                z = jnp.where(u == 0, pltpu.roll(f, c - 1, axis=0),
                              jnp.where(u == 1, 1.0, jnp.where(u == 2, f, f * pltpu.roll(f, 1, axis=0))))
            else:
                z = jnp.where(row % 2 == 1, f, 1.0)
            z = z.astype(BF16)
            qz = q16 * z
            kz = k16 * z
            sel = ((txs // half) == 1) & causal
            for h in range(HGRN_HEADS):
                cols = slice(h * dk, (h + 1) * dk)
                gm = lax.dot_general(qz[:, cols], kz[:, cols], nt, preferred_element_type=F32)
                a_mats[h] = jnp.where(sel, gm, a_mats[h])
            m = half

        q_dec = q16 * jnp.exp2(b).astype(BF16)
        k_dec = k16 * jnp.exp2(b_last - b).astype(BF16)
        decay_last = jnp.exp2(b_last)
        gain = gain_ref[...].astype(F32)
        hg = hg_ref[rows, :].astype(F32)
        gate = hg / (1.0 + jnp.exp(-hg))
        for h in range(HGRN_HEADS):
            kc = slice(h * dk, (h + 1) * dk)
            vc = slice(h * dv, (h + 1) * dv)
            st = st_ref[h]
            o = jnp.dot(a_mats[h].astype(BF16), v[:, vc], preferred_element_type=F32)
            o = o + lax.dot_general(q_dec[:, kc], st.astype(BF16), nt, preferred_element_type=F32)
            st_ref[h] = st * decay_last[:, kc] + lax.dot_general(v[:, vc], k_dec[:, kc], tn,
                                                                  preferred_element_type=F32)
            ms = jnp.mean(o * o, axis=-1, keepdims=True)
            y = (o * lax.rsqrt(ms + NORM_EPS)) * gain[:, vc]
            o_ref[rows, vc] = (y * gate[:, vc]).astype(o_ref.dtype)


def _hgrn(proj3, lb_params, gain_row, layer, *, chunk=128, n_chunks=2):
    b, t, _ = proj3.shape
    rows = chunk * n_chunks
    assert t % rows == 0 and chunk & (chunk - 1) == 0
    w = HGRN_WIDTH
    tril = jnp.asarray(np.tril(np.ones((chunk, chunk), np.float32)), dtype=BF16)
    seg = lambda name: _segment_specs(name, (None, rows), lambda bi, i: (bi, i))
    in_specs = [pl.BlockSpec(lb_params.shape, lambda bi, i: (0, 0)),
                pl.BlockSpec((1, w), lambda bi, i: (0, 0)),
                pl.BlockSpec((chunk, chunk), lambda bi, i: (0, 0)),
                *seg("hq"), *seg("hf"), *seg("hi"), *seg("hg")]
    return pl.pallas_call(
        functools.partial(_hgrn_kernel, layer=layer, chunk=chunk, n_chunks=n_chunks),
        grid=(b, t // rows),
        in_specs=in_specs,
        out_specs=pl.BlockSpec((None, rows, w), lambda bi, i: (bi, i, 0)),
        out_shape=jax.ShapeDtypeStruct((b, t, w), BF16),
        scratch_shapes=[pltpu.VMEM((HGRN_HEADS, HGRN_VALUE_DIM, HGRN_KEY_DIM), F32)],
        compiler_params=pltpu.CompilerParams(
            dimension_semantics=("arbitrary", "arbitrary"), vmem_limit_bytes=VMEM_LIMIT),
        name="hgrn2",
    )(lb_params, gain_row, tril, *([proj3] * (len(in_specs) - 3)))


def _rope_tables(pos_ref, inv_freq_row):
    n_freq = ATTN_HEAD_DIM // 2
    n_slab = LANES // n_freq
    slab = pos_ref.shape[0] // n_slab
    assert slab * n_slab == pos_ref.shape[0]
    group = lax.broadcasted_iota(jnp.int32, (1, LANES), 1) // n_freq
    packed = jnp.zeros((slab, LANES), F32)
    for i in range(n_slab):
        packed = jnp.where(group == i, pos_ref[i * slab:(i + 1) * slab, :].astype(F32), packed)
    ang = packed * inv_freq_row
    cos_p, sin_p = jnp.cos(ang), jnp.sin(ang)

    def spread(t, i):
        t = jnp.where(group == i, t, 0.0)
        t = t + pltpu.roll(t, n_freq, axis=1)
        return t + pltpu.roll(t, 2 * n_freq, axis=1)

    assert n_slab == 4
    cos = jnp.concatenate([spread(cos_p, i) for i in range(n_slab)], axis=0)
    sin = jnp.concatenate([spread(sin_p, i) for i in range(n_slab)], axis=0)
    lane = lax.broadcasted_iota(jnp.int32, (1, LANES), 1)
    first = (lane % ATTN_HEAD_DIM) < n_freq
    return cos, jnp.where(first, -sin, 0.0), jnp.where(first, 0.0, sin)


def _rope(x, cos, sin_first, sin_second):
    half = ATTN_HEAD_DIM // 2
    from_right = pltpu.roll(x, LANES - half, axis=1)
    from_left = pltpu.roll(x, half, axis=1)
    return x * cos + from_right * sin_first + from_left * sin_second


class _AttnBlock:
    def __init__(self, first_block, sink_ref, invf_ref, pos_ref, q_ref, g_ref, k_ref, v_ref, o_ref, kpad_ref,
                 vpad_ref, *, tq):
        self.first_block, self.sink_ref, self.invf_ref, self.pos_ref = first_block, sink_ref, invf_ref, pos_ref
        self.q_ref, self.g_ref, self.k_ref, self.v_ref, self.o_ref = q_ref, g_ref, k_ref, v_ref, o_ref
        self.kpad_ref, self.vpad_ref, self.tq = kpad_ref, vpad_ref, tq
        self.items = [(sub, g) for sub in range(tq // WINDOW) for g in range(ATTN_KV_HEADS)]
        self.lo_lane = lax.broadcasted_iota(jnp.int32, (1, LANES), 1) < ATTN_HEAD_DIM

    def prepare(self):
        blk, tq, kpad_ref, vpad_ref, lo_lane = WINDOW, self.tq, self.kpad_ref, self.vpad_ref, self.lo_lane

        @pl.when(self.first_block)
        def _():
            kpad_ref[:, :blk, :] = jnp.zeros((2 * ATTN_KV_HEADS, blk, LANES), BF16)
            vpad_ref[:, :blk, :] = jnp.zeros((2 * ATTN_KV_HEADS, blk, LANES), BF16)

        @pl.when(jnp.logical_not(self.first_block))
        def _():
            kpad_ref[:, :blk, :] = kpad_ref[:, tq:, :]
            vpad_ref[:, :blk, :] = vpad_ref[:, tq:, :]

        self.trig = _rope_tables(self.pos_ref, self.invf_ref[...])
        for s in range(KV_WIDTH // LANES):
            cols = slice(s * LANES, (s + 1) * LANES)
            kk = _rope(self.k_ref[:, cols].astype(F32), *self.trig)
            vv = self.v_ref[:, cols].astype(F32)
            kk_sw = pltpu.roll(kk, ATTN_HEAD_DIM, axis=1)
            vv_sw = pltpu.roll(vv, ATTN_HEAD_DIM, axis=1)
            for ref, own, swapped in ((kpad_ref, kk, kk_sw), (vpad_ref, vv, vv_sw)):
                ref[4 * s + 0, blk:, :] = jnp.where(lo_lane, own, 0.0).astype(BF16)
                ref[4 * s + 1, blk:, :] = jnp.where(lo_lane, 0.0, swapped).astype(BF16)
                ref[4 * s + 2, blk:, :] = jnp.where(lo_lane, swapped, 0.0).astype(BF16)
                ref[4 * s + 3, blk:, :] = jnp.where(lo_lane, 0.0, own).astype(BF16)

    def scores(self, item):
        sub, g = item
        blk = WINDOW
        rows = slice(sub * blk, (sub + 1) * blk)
        keys = slice(sub * blk, sub * blk + 2 * blk)
        scale = (ATTN_HEAD_DIM ** -0.5) * LOG2_E
        nt = (((1,), (1,)), ((), ()))
        q2 = jnp.concatenate(
            [_rope(self.q_ref[rows, p * LANES:(p + 1) * LANES].astype(F32), *(t[rows] for t in self.trig))
             for p in (2 * g, 2 * g + 1)], axis=0)
        q2 = (q2 * scale).astype(BF16)
        return [lax.dot_general(q2, self.kpad_ref[2 * g + which, keys, :], nt, preferred_element_type=F32)
                for which in range(2)]

    def softmax(self, item, s_pair):
        sub, g = item
        blk = WINDOW
        n_stack = ATTN_GROUP // 2
        r_idx = lax.broadcasted_iota(jnp.int32, (blk, 2 * blk), 0)
        c_idx = lax.broadcasted_iota(jnp.int32, (blk, 2 * blk), 1)
        mask = (c_idx > r_idx) & (c_idx <= r_idx + blk)
        if sub == 0:
            mask = mask & (c_idx >= jnp.where(self.first_block, blk, 0))
        mask = jnp.concatenate([mask] * n_stack, axis=0)
        lo_rows = lax.broadcasted_iota(jnp.int32, (n_stack * blk, 1), 0) < blk
        out = []
        for which, s in enumerate(s_pair):
            s = jnp.where(mask, s, -jnp.inf)
            sink = jnp.where(lo_rows, self.sink_ref[4 * g + which], self.sink_ref[4 * g + 2 + which]) * LOG2_E
            m = jnp.maximum(jnp.max(s, axis=-1, keepdims=True), sink)
            p = jnp.exp2(s - m)
            denom = jnp.sum(p, axis=-1, keepdims=True) + jnp.exp2(sink - m)
            out.append((p.astype(BF16), 1.0 / denom))
        return out

    def values(self, item, pw):
        sub, g = item
        keys = slice(sub * WINDOW, sub * WINDOW + 2 * WINDOW)
        acc = jnp.dot(pw[0][0], self.vpad_ref[2 * g, keys, :], preferred_element_type=F32)
        acc = acc + jnp.dot(pw[1][0], self.vpad_ref[2 * g + 1, keys, :], preferred_element_type=F32)
        return acc * jnp.where(self.lo_lane, pw[0][1], pw[1][1])

    def store(self, item, attn):
        sub, g = item
        blk = WINDOW
        rows = slice(sub * blk, (sub + 1) * blk)
        for i, p in enumerate((2 * g, 2 * g + 1)):
            cols = slice(p * LANES, (p + 1) * LANES)
            gate = self.g_ref[rows, cols].astype(F32)
            self.o_ref[rows, cols] = (attn[i * blk:(i + 1) * blk] * (gate / (1.0 + jnp.exp(-gate)))
                                      ).astype(self.o_ref.dtype)


def _attn_merge_kernel(sink_ref, invf_ref, pos_ref, *refs, tm, blocks_per_seq, final_norm):
    nq = ATTN_WIDTH // _COL_BLOCK
    q_ref, g_ref = _Cols(refs[:nq]), _Cols(refs[nq:2 * nq])
    k_ref, v_ref, gh_ref = refs[2 * nq:2 * nq + 3]
    rest = refs[2 * nq + 3:]
    ng = (len(rest) - 10) // 2
    ma_ref, mh_ref = _Cols(rest[:ng]), _Cols(rest[ng:2 * ng])
    x_ref, wa_ref, wh_ref, wo_ref, fg_ref, o_ref, kpad_ref, vpad_ref, ga_cur, ga_next = rest[2 * ng:]
    i = pl.program_id(0)
    first_block = lax.rem(i, blocks_per_seq) == 0

    @pl.when(i == 0)
    def _():
        ga_cur[...] = jnp.zeros_like(ga_cur)

    attn = _AttnBlock(first_block, sink_ref, invf_ref, pos_ref, q_ref, g_ref, k_ref, v_ref, ga_next, kpad_ref,
                      vpad_ref, tq=tm)
    items = attn.items
    half = len(items) // 2
    attn.prepare()
    scores = [attn.scores(it) for it in items]
    ya = jnp.dot(ga_cur[...], wa_ref[...], preferred_element_type=F32)
    probs = [attn.softmax(it, s) for it, s in zip(items[:half], scores[:half])]
    yh = jnp.dot(gh_ref[...], wh_ref[...], preferred_element_type=F32)
    probs += [attn.softmax(it, s) for it, s in zip(items[half:], scores[half:])]
    outs = [attn.values(it, pw) for it, pw in zip(items, probs)]
    merged = (_sigmoid(ma_ref[:, :].astype(F32)) * ya + _sigmoid(mh_ref[:, :].astype(F32)) * yh).astype(BF16)
    y = x_ref[...] + jnp.dot(merged, wo_ref[...], preferred_element_type=F32)
    for it, o in zip(items, outs):
        attn.store(it, o)
    if final_norm:
        ms = jnp.mean(y * y, axis=-1, keepdims=True)
        y = (y * lax.rsqrt(ms + NORM_EPS)) * fg_ref[...]
    o_ref[...] = y
    ga_cur[...] = ga_next[...]


def _attn_merge(proj2, pos2, sinks, inv_freq_row, gh, x2d, wa, wh, wo, fgain, final_norm, *, seq_len, tm=256):
    n_tok, d = x2d.shape
    assert n_tok % tm == 0 and seq_len % tm == 0 and tm % WINDOW == 0
    nb = n_tok // tm
    cur = lambda i: (jnp.minimum(i, nb - 1),)
    prv = lambda i: (jnp.maximum(i - 1, 0),)
    const = lambda shape: pl.BlockSpec(shape, lambda i: (0, 0), pipeline_mode=pl.Buffered(1))
    seg_c = lambda name: _segment_specs(name, (tm,), cur)
    seg_p = lambda name: _segment_specs(name, (tm,), prv)
    attn_specs = [*seg_c("aq"), *seg_c("ag"), *seg_c("ak"), *seg_c("av")]
    gate_specs = [*seg_p("ma"), *seg_p("mh")]
    in_specs = [pl.BlockSpec(memory_space=pltpu.SMEM),
                pl.BlockSpec((1, LANES), lambda i: (0, 0)),
                pl.BlockSpec((tm, 1), lambda i: cur(i) + (0,)),
                *attn_specs,
                pl.BlockSpec((tm, HGRN_WIDTH), lambda i: prv(i) + (0,)),
                *gate_specs,
                pl.BlockSpec((tm, d), lambda i: prv(i) + (0,)),
                const(wa.shape), const(wh.shape), const(wo.shape), const((1, d))]
    return pl.pallas_call(
        functools.partial(_attn_merge_kernel, tm=tm, blocks_per_seq=seq_len // tm, final_norm=final_norm),
        grid=(nb + 1,),
        in_specs=in_specs,
        out_specs=pl.BlockSpec((tm, d), lambda i: prv(i) + (0,)),
        out_shape=jax.ShapeDtypeStruct((n_tok, d), F32),
        scratch_shapes=[pltpu.VMEM((2 * ATTN_KV_HEADS, WINDOW + tm, LANES), BF16),
                        pltpu.VMEM((2 * ATTN_KV_HEADS, WINDOW + tm, LANES), BF16),
                        pltpu.VMEM((tm, ATTN_WIDTH), BF16),
                        pltpu.VMEM((tm, ATTN_WIDTH), BF16)],
        compiler_params=pltpu.CompilerParams(
            dimension_semantics=("arbitrary",), vmem_limit_bytes=VMEM_LIMIT),
        name="attention_merge",
    )(sinks, inv_freq_row, pos2, *([proj2] * len(attn_specs)), gh, *([proj2] * len(gate_specs)), x2d, wa, wh, wo,
      fgain.reshape(1, d))


def kernel(x, positions, norm_gain, w_in, attn_sinks, hgrn_lower_bounds, hgrn_norm_gain, w_attn_out,
           w_hgrn_out, w_o, final_norm_gain):
    b, t, d = x.shape
    depth = w_in.shape[0]
    assert w_in.shape[2] == IN_WIDTH and _SEG_OFF["ma"][1] == d
    half = ATTN_HEAD_DIM // 2
    inv_freq = ROPE_THETA ** (-jnp.arange(half, dtype=F32) / half)
    inv_freq_row = jnp.tile(inv_freq, LANES // half).reshape(1, LANES)
    pos2 = positions.astype(jnp.int32).reshape(b * t, 1)

    x2d = x.reshape(b * t, d)
    for l in range(depth):
        proj = _in_projection(x2d, norm_gain[l], w_in[l].astype(BF16))
        gh = _hgrn(proj.reshape(b, t, IN_WIDTH), hgrn_lower_bounds.astype(F32),
                   hgrn_norm_gain[l].reshape(1, HGRN_WIDTH).astype(F32), l)
        x2d = _attn_merge(proj, pos2, attn_sinks[l].astype(F32), inv_freq_row, gh.reshape(b * t, HGRN_WIDTH), x2d,
                          w_attn_out[l].astype(BF16), w_hgrn_out[l].astype(BF16), w_o[l].astype(BF16),
                          final_norm_gain, l == depth - 1, seq_len=t)
    return x2d.reshape(b, t, d)
```

```python
import functools

import numpy as np
import jax
import jax.numpy as jnp
from jax import lax
from jax.experimental import pallas as pl
from jax.experimental.pallas import tpu as pltpu

F32 = jnp.float32
BF16 = jnp.bfloat16

ATTN_HEADS = 16
ATTN_KV_HEADS = 4
ATTN_HEAD_DIM = 64
ATTN_GROUP = ATTN_HEADS // ATTN_KV_HEADS
ATTN_WIDTH = ATTN_HEADS * ATTN_HEAD_DIM
KV_WIDTH = ATTN_KV_HEADS * ATTN_HEAD_DIM
WINDOW = 128
ROPE_THETA = 10000.0
HGRN_HEADS = 8
HGRN_KEY_DIM = 128
HGRN_VALUE_DIM = 128
HGRN_WIDTH = HGRN_HEADS * HGRN_VALUE_DIM
NORM_EPS = 1e-6
LOG2_E = 1.4426950408889634

LANES = 128
VMEM_LIMIT = 56 * 1024 * 1024

_SEGMENTS = (("aq", 1024), ("ak", 256), ("av", 256), ("ag", 1024), ("hq", 1024),
             ("hf", 1024), ("hi", 1024), ("hg", 1024), ("ma", 2048), ("mh", 2048))
_SEG_OFF = {}
IN_WIDTH = 0
for _name, _w in _SEGMENTS:
    _SEG_OFF[_name] = (IN_WIDTH, _w)
    IN_WIDTH += _w
_COL_BLOCK = 512


def _segment_specs(name, lead_block, index_fn):
    off, width = _SEG_OFF[name]
    cb = min(_COL_BLOCK, width)
    assert off % cb == 0 and width % cb == 0
    return [pl.BlockSpec(lead_block + (cb,), lambda *ids, c=(off + i * cb) // cb: index_fn(*ids) + (c,))
            for i in range(width // cb)]


class _Cols:
    def __init__(self, refs):
        self._refs = refs
        self._cb = refs[0].shape[-1]

    def __getitem__(self, idx):
        rows, cols = idx
        start = 0 if cols.start is None else cols.start
        stop = self._cb * len(self._refs) if cols.stop is None else cols.stop
        pieces, c = [], start
        while c < stop:
            i = c // self._cb
            hi = min(stop, (i + 1) * self._cb)
            pieces.append(self._refs[i][rows, c - i * self._cb:hi - i * self._cb])
            c = hi
        return pieces[0] if len(pieces) == 1 else jnp.concatenate(pieces, axis=1)


class _Shifted:
    def __init__(self, ref, off, width):
        self._ref, self._off, self._width = ref, off, width

    def __getitem__(self, idx):
        rows, cols = idx
        start = 0 if cols.start is None else cols.start
        stop = self._width if cols.stop is None else cols.stop
        return self._ref[rows, self._off + start:self._off + stop]


def _sigmoid(x):
    return 1.0 / (1.0 + jnp.exp(-x))


def _inproj_kernel(x_ref, g_ref, w_ref, o_ref, h_ref, *, row_chunk):
    @pl.when(pl.program_id(1) == 0)
    def _():
        def body(r, carry):
            rows = pl.ds(pl.multiple_of(r * row_chunk, row_chunk), row_chunk)
            x = x_ref[rows, :]
            ms = jnp.mean(x * x, axis=-1, keepdims=True)
            h_ref[rows, :] = ((x * lax.rsqrt(ms + NORM_EPS)) * g_ref[...]).astype(BF16)
            return carry
        lax.fori_loop(0, x_ref.shape[0] // row_chunk, body, 0)

    o_ref[...] = jnp.dot(h_ref[...], w_ref[...], preferred_element_type=F32).astype(o_ref.dtype)


def _in_projection(x2d, gain, w_bf16, *, tm=1024, tn=1792):
    n_tok, d = x2d.shape
    n_out = w_bf16.shape[1]
    assert n_tok % tm == 0 and n_out % tn == 0
    return pl.pallas_call(
        functools.partial(_inproj_kernel, row_chunk=128),
        grid=(n_tok // tm, n_out // tn),
        in_specs=[pl.BlockSpec((tm, d), lambda i, j: (i, 0)),
                  pl.BlockSpec((1, d), lambda i, j: (0, 0)),
                  pl.BlockSpec((d, tn), lambda i, j: (0, j))],
        out_specs=pl.BlockSpec((tm, tn), lambda i, j: (i, j)),
        out_shape=jax.ShapeDtypeStruct((n_tok, n_out), BF16),
        scratch_shapes=[pltpu.VMEM((tm, d), BF16)],
        compiler_params=pltpu.CompilerParams(
            dimension_semantics=("arbitrary", "arbitrary"), vmem_limit_bytes=VMEM_LIMIT),
        name="in_projection",
    )(x2d, gain.reshape(1, d), w_bf16)


def _anchor_rows(b, m):
    c, w = b.shape
    g = c // m
    b3 = b.reshape(g, m, w)
    return jnp.broadcast_to(b3[:, m // 2 - 1:m // 2, :], (g, m, w)).reshape(c, w)


def _hgrn_kernel(lbp_ref, gain_ref, tril_ref, *refs, layer, chunk, n_chunks):
    ns = HGRN_WIDTH // _COL_BLOCK
    hq_ref, hf_ref, hi_ref, hg_ref = (_Cols(refs[i * ns:(i + 1) * ns]) for i in range(4))
    o_ref, st_ref = refs[4 * ns:]
    c = chunk
    dk, dv = HGRN_KEY_DIM, HGRN_VALUE_DIM
    w = HGRN_HEADS * dk

    @pl.when(pl.program_id(1) == 0)
    def _():
        st_ref[...] = jnp.zeros_like(st_ref)

    lbp = lbp_ref[...].astype(F32)
    e = jnp.exp(lbp - jnp.max(lbp, axis=0, keepdims=True))
    lb = jnp.sum(e[:layer + 1], axis=0, keepdims=True) / jnp.sum(e, axis=0, keepdims=True)

    for ci in range(n_chunks):
        rows = slice(ci * c, (ci + 1) * c)
        k = (1.0 - lb) / (1.0 + jnp.exp(hf_ref[rows, :].astype(F32)))
        log2_f = jnp.log(1.0 - k) * LOG2_E
        k16 = k.astype(BF16)
        hq = hq_ref[rows, :].astype(F32)
        q16 = ((hq / (1.0 + jnp.exp(-hq))) * (dk ** -0.5)).astype(BF16)
        v = hi_ref[rows, :]

        f_hi = log2_f.astype(BF16)
        f_lo = (log2_f - f_hi.astype(F32)).astype(BF16)
        cs = jnp.dot(tril_ref[...], jnp.concatenate([f_hi, f_lo], axis=1), preferred_element_type=F32)
        b = cs[:, :w] + cs[:, w:]
        b_last = b[c - 1:c, :]

        ti = lax.broadcasted_iota(jnp.int32, (c, c), 0)
        si = lax.broadcasted_iota(jnp.int32, (c, c), 1)
        txs = ti ^ si
        causal = ti > si
        nt = (((1,), (1,)), ((), ()))
        tn = (((0,), (0,)), ((), ()))

        a_mats = [None] * HGRN_HEADS
        for h in range(HGRN_HEADS):
            cols = slice(h * dk, (h + 1) * dk)
            gm = lax.dot_general(q16[:, cols], k16[:, cols], nt, preferred_element_type=F32)
            a_mats[h] = jnp.where(txs == 0, gm, 0.0)
        f = 1.0 - k
        row = lax.broadcasted_iota(jnp.int32, (c, 1), 0)
        m = c
        while m >= 2:
            half = m // 2
            if m > 4:
                z = jnp.exp2(-jnp.abs(b - _anchor_rows(b, m)))
            elif m == 4:
                u = row % 4
                z = jnp.where(u == 0, pltpu.roll(f, c - 1, axis=0),
                              jnp.where(u == 1, 1.0, jnp.where(u == 2, f, f * pltpu.roll(f, 1, axis=0))))
            else:
                z = jnp.where(row % 2 == 1, f, 1.0)
            z = z.astype(BF16)
            qz = q16 * z
            kz = k16 * z
            sel = ((txs // half) == 1) & causal
            for h in range(HGRN_HEADS):
                cols = slice(h * dk, (h + 1) * dk)
                gm = lax.dot_general(qz[:, cols], kz[:, cols], nt, preferred_element_type=F32)
                a_mats[h] = jnp.where(sel, gm, a_mats[h])
            m = half

        q_dec = q16 * jnp.exp2(b).astype(BF16)
        k_dec = k16 * jnp.exp2(b_last - b).astype(BF16)
        decay_last = jnp.exp2(b_last)
        gain = gain_ref[...].astype(F32)
        hg = hg_ref[rows, :].astype(F32)
        gate = hg / (1.0 + jnp.exp(-hg))
        for h in range(HGRN_HEADS):
            kc = slice(h * dk, (h + 1) * dk)
            vc = slice(h * dv, (h + 1) * dv)
            st = st_ref[h]
            o = jnp.dot(a_mats[h].astype(BF16), v[:, vc], preferred_element_type=F32)
            o = o + lax.dot_general(q_dec[:, kc], st.astype(BF16), nt, preferred_element_type=F32)
            st_ref[h] = st * decay_last[:, kc] + lax.dot_general(v[:, vc], k_dec[:, kc], tn,
                                                                  preferred_element_type=F32)
            ms = jnp.mean(o * o, axis=-1, keepdims=True)
            y = (o * lax.rsqrt(ms + NORM_EPS)) * gain[:, vc]
            o_ref[rows, vc] = (y * gate[:, vc]).astype(o_ref.dtype)


def _hgrn(proj3, lb_params, gain_row, layer, *, chunk=128, n_chunks=4):
    b, t, _ = proj3.shape
    rows = chunk * n_chunks
    assert t % rows == 0 and chunk & (chunk - 1) == 0
    w = HGRN_WIDTH
    tril = jnp.asarray(np.tril(np.ones((chunk, chunk), np.float32)), dtype=BF16)
    seg = lambda name: _segment_specs(name, (None, rows), lambda bi, i: (bi, i))
    in_specs = [pl.BlockSpec(lb_params.shape, lambda bi, i: (0, 0)),
                pl.BlockSpec((1, w), lambda bi, i: (0, 0)),
                pl.BlockSpec((chunk, chunk), lambda bi, i: (0, 0)),
                *seg("hq"), *seg("hf"), *seg("hi"), *seg("hg")]
    return pl.pallas_call(
        functools.partial(_hgrn_kernel, layer=layer, chunk=chunk, n_chunks=n_chunks),
        grid=(b, t // rows),
        in_specs=in_specs,
        out_specs=pl.BlockSpec((None, rows, w), lambda bi, i: (bi, i, 0)),
        out_shape=jax.ShapeDtypeStruct((b, t, w), BF16),
        scratch_shapes=[pltpu.VMEM((HGRN_HEADS, HGRN_VALUE_DIM, HGRN_KEY_DIM), F32)],
        compiler_params=pltpu.CompilerParams(
            dimension_semantics=("arbitrary", "arbitrary"), vmem_limit_bytes=VMEM_LIMIT),
        name="hgrn2",
    )(lb_params, gain_row, tril, *([proj3] * (len(in_specs) - 3)))


def _rope_tables(pos_ref, inv_freq_row):
    n_freq = ATTN_HEAD_DIM // 2
    n_slab = LANES // n_freq
    slab = pos_ref.shape[0] // n_slab
    assert slab * n_slab == pos_ref.shape[0]
    group = lax.broadcasted_iota(jnp.int32, (1, LANES), 1) // n_freq
    packed = jnp.zeros((slab, LANES), F32)
    for i in range(n_slab):
        packed = jnp.where(group == i, pos_ref[i * slab:(i + 1) * slab, :].astype(F32), packed)
    ang = packed * inv_freq_row
    cos_p, sin_p = jnp.cos(ang), jnp.sin(ang)

    def spread(t, i):
        t = jnp.where(group == i, t, 0.0)
        t = t + pltpu.roll(t, n_freq, axis=1)
        return t + pltpu.roll(t, 2 * n_freq, axis=1)

    assert n_slab == 4
    cos = jnp.concatenate([spread(cos_p, i) for i in range(n_slab)], axis=0)
    sin = jnp.concatenate([spread(sin_p, i) for i in range(n_slab)], axis=0)
    lane = lax.broadcasted_iota(jnp.int32, (1, LANES), 1)
    first = (lane % ATTN_HEAD_DIM) < n_freq
    return cos, jnp.where(first, -sin, 0.0), jnp.where(first, 0.0, sin)


def _rope(x, cos, sin_first, sin_second):
    half = ATTN_HEAD_DIM // 2
    from_right = pltpu.roll(x, LANES - half, axis=1)
    from_left = pltpu.roll(x, half, axis=1)
    return x * cos + from_right * sin_first + from_left * sin_second


class _AttnBlock:
    def __init__(self, first_block, sink_ref, invf_ref, pos_ref, q_ref, g_ref, k_ref, v_ref, o_ref, kpad_ref,
                 vpad_ref, *, tq):
        self.first_block, self.sink_ref, self.invf_ref, self.pos_ref = first_block, sink_ref, invf_ref, pos_ref
        self.q_ref, self.g_ref, self.k_ref, self.v_ref, self.o_ref = q_ref, g_ref, k_ref, v_ref, o_ref
        self.kpad_ref, self.vpad_ref, self.tq = kpad_ref, vpad_ref, tq
        self.items = [(sub, g) for sub in range(tq // WINDOW) for g in range(ATTN_KV_HEADS)]
        self.lo_lane = lax.broadcasted_iota(jnp.int32, (1, LANES), 1) < ATTN_HEAD_DIM

    def prepare(self):
        blk, tq, kpad_ref, vpad_ref, lo_lane = WINDOW, self.tq, self.kpad_ref, self.vpad_ref, self.lo_lane

        @pl.when(self.first_block)
        def _():
            kpad_ref[:, :blk, :] = jnp.zeros((2 * ATTN_KV_HEADS, blk, LANES), BF16)
            vpad_ref[:, :blk, :] = jnp.zeros((2 * ATTN_KV_HEADS, blk, LANES), BF16)

        @pl.when(jnp.logical_not(self.first_block))
        def _():
            kpad_ref[:, :blk, :] = kpad_ref[:, tq:, :]
            vpad_ref[:, :blk, :] = vpad_ref[:, tq:, :]

        self.trig = _rope_tables(self.pos_ref, self.invf_ref[...])
        for s in range(KV_WIDTH // LANES):
            cols = slice(s * LANES, (s + 1) * LANES)
            kk = _rope(self.k_ref[:, cols].astype(F32), *self.trig)
            vv = self.v_ref[:, cols].astype(F32)
            kk_sw = pltpu.roll(kk, ATTN_HEAD_DIM, axis=1)
            vv_sw = pltpu.roll(vv, ATTN_HEAD_DIM, axis=1)
            for ref, own, swapped in ((kpad_ref, kk, kk_sw), (vpad_ref, vv, vv_sw)):
                ref[4 * s + 0, blk:, :] = jnp.where(lo_lane, own, 0.0).astype(BF16)
                ref[4 * s + 1, blk:, :] = jnp.where(lo_lane, 0.0, swapped).astype(BF16)
                ref[4 * s + 2, blk:, :] = jnp.where(lo_lane, swapped, 0.0).astype(BF16)
                ref[4 * s + 3, blk:, :] = jnp.where(lo_lane, 0.0, own).astype(BF16)

    def scores(self, item):
        sub, g = item
        blk = WINDOW
        rows = slice(sub * blk, (sub + 1) * blk)
        keys = slice(sub * blk, sub * blk + 2 * blk)
        scale = (ATTN_HEAD_DIM ** -0.5) * LOG2_E
        nt = (((1,), (1,)), ((), ()))
        q2 = jnp.concatenate(
            [_rope(self.q_ref[rows, p * LANES:(p + 1) * LANES].astype(F32), *(t[rows] for t in self.trig))
             for p in (2 * g, 2 * g + 1)], axis=0)
        q2 = (q2 * scale).astype(BF16)
        return [lax.dot_general(q2, self.kpad_ref[2 * g + which, keys, :], nt, preferred_element_type=F32)
                for which in range(2)]

    def softmax(self, item, s_pair):
        sub, g = item
        blk = WINDOW
        n_stack = ATTN_GROUP // 2
        r_idx = lax.broadcasted_iota(jnp.int32, (blk, 2 * blk), 0)
        c_idx = lax.broadcasted_iota(jnp.int32, (blk, 2 * blk), 1)
        mask = (c_idx > r_idx) & (c_idx <= r_idx + blk)
        if sub == 0:
            mask = mask & (c_idx >= jnp.where(self.first_block, blk, 0))
        mask = jnp.concatenate([mask] * n_stack, axis=0)
        lo_rows = lax.broadcasted_iota(jnp.int32, (n_stack * blk, 1), 0) < blk
        out = []
        for which, s in enumerate(s_pair):
            s = jnp.where(mask, s, -jnp.inf)
            sink = jnp.where(lo_rows, self.sink_ref[4 * g + which], self.sink_ref[4 * g + 2 + which]) * LOG2_E
            m = jnp.maximum(jnp.max(s, axis=-1, keepdims=True), sink)
            p = jnp.exp2(s - m)
            denom = jnp.sum(p, axis=-1, keepdims=True) + jnp.exp2(sink - m)
            out.append((p.astype(BF16), 1.0 / denom))
        return out

    def values(self, item, pw):
        sub, g = item
        keys = slice(sub * WINDOW, sub * WINDOW + 2 * WINDOW)
        acc = jnp.dot(pw[0][0], self.vpad_ref[2 * g, keys, :], preferred_element_type=F32)
        acc = acc + jnp.dot(pw[1][0], self.vpad_ref[2 * g + 1, keys, :], preferred_element_type=F32)
        return acc * jnp.where(self.lo_lane, pw[0][1], pw[1][1])

    def store(self, item, attn):
        sub, g = item
        blk = WINDOW
        rows = slice(sub * blk, (sub + 1) * blk)
        for i, p in enumerate((2 * g, 2 * g + 1)):
            cols = slice(p * LANES, (p + 1) * LANES)
            gate = self.g_ref[rows, cols].astype(F32)
            self.o_ref[rows, cols] = (attn[i * blk:(i + 1) * blk] * (gate / (1.0 + jnp.exp(-gate)))
                                      ).astype(self.o_ref.dtype)


def _attn_merge_kernel(sink_ref, invf_ref, pos_ref, *refs, tm, blocks_per_seq, final_norm):
    a_ref, gh_ref = refs[:2]
    q_ref, k_ref, v_ref, g_ref = (_Shifted(a_ref, *_SEG_OFF[n]) for n in ("aq", "ak", "av", "ag"))
    rest = refs[2:]
    ng = (len(rest) - 10) // 2
    ma_ref, mh_ref = _Cols(rest[:ng]), _Cols(rest[ng:2 * ng])
    x_ref, wa_ref, wh_ref, wo_ref, fg_ref, o_ref, kpad_ref, vpad_ref, ga_cur, ga_next = rest[2 * ng:]
    i = pl.program_id(0)
    first_block = lax.rem(i, blocks_per_seq) == 0

    @pl.when(i == 0)
    def _():
        ga_cur[...] = jnp.zeros_like(ga_cur)

    attn = _AttnBlock(first_block, sink_ref, invf_ref, pos_ref, q_ref, g_ref, k_ref, v_ref, ga_next, kpad_ref,
                      vpad_ref, tq=tm)
    items = attn.items
    half = len(items) // 2
    attn.prepare()
    scores = [attn.scores(it) for it in items]
    ya = jnp.dot(ga_cur[...], wa_ref[...], preferred_element_type=F32)
    probs = [attn.softmax(it, s) for it, s in zip(items[:half], scores[:half])]
    yh = jnp.dot(gh_ref[...], wh_ref[...], preferred_element_type=F32)
    probs += [attn.softmax(it, s) for it, s in zip(items[half:], scores[half:])]
    outs = [attn.values(it, pw) for it, pw in zip(items, probs)]
    merged = (_sigmoid(ma_ref[:, :].astype(F32)) * ya + _sigmoid(mh_ref[:, :].astype(F32)) * yh).astype(BF16)
    y = x_ref[...] + jnp.dot(merged, wo_ref[...], preferred_element_type=F32)
    for it, o in zip(items, outs):
        attn.store(it, o)
    if final_norm:
        ms = jnp.mean(y * y, axis=-1, keepdims=True)
        y = (y * lax.rsqrt(ms + NORM_EPS)) * fg_ref[...]
    o_ref[...] = y
    ga_cur[...] = ga_next[...]


def _attn_merge(proj2, pos2, sinks, inv_freq_row, gh, x2d, wa, wh, wo, fgain, final_norm, *, seq_len, tm=256):
    n_tok, d = x2d.shape
    assert n_tok % tm == 0 and seq_len % tm == 0 and tm % WINDOW == 0
    nb = n_tok // tm
    cur = lambda i: (jnp.minimum(i, nb - 1),)
    prv = lambda i: (jnp.maximum(i - 1, 0),)
    const = lambda shape: pl.BlockSpec(shape, lambda i: (0, 0), pipeline_mode=pl.Buffered(1))
    seg_p = lambda name: _segment_specs(name, (tm,), prv)
    attn_width = _SEG_OFF["ag"][0] + _SEG_OFF["ag"][1]
    assert _SEG_OFF["aq"][0] == 0 and all(_SEG_OFF[n][0] + _SEG_OFF[n][1] <= attn_width for n in ("ak", "av"))
    gate_specs = [*seg_p("ma"), *seg_p("mh")]
    in_specs = [pl.BlockSpec(memory_space=pltpu.SMEM),
                pl.BlockSpec((1, LANES), lambda i: (0, 0)),
                pl.BlockSpec((tm, 1), lambda i: cur(i) + (0,)),
                pl.BlockSpec((tm, attn_width), lambda i: cur(i) + (0,)),
                pl.BlockSpec((tm, HGRN_WIDTH), lambda i: prv(i) + (0,)),
                *gate_specs,
                pl.BlockSpec((tm, d), lambda i: prv(i) + (0,)),
                const(wa.shape), const(wh.shape), const(wo.shape), const((1, d))]
    return pl.pallas_call(
        functools.partial(_attn_merge_kernel, tm=tm, blocks_per_seq=seq_len // tm, final_norm=final_norm),
        grid=(nb + 1,),
        in_specs=in_specs,
        out_specs=pl.BlockSpec((tm, d), lambda i: prv(i) + (0,)),
        out_shape=jax.ShapeDtypeStruct((n_tok, d), F32),
        scratch_shapes=[pltpu.VMEM((2 * ATTN_KV_HEADS, WINDOW + tm, LANES), BF16),
                        pltpu.VMEM((2 * ATTN_KV_HEADS, WINDOW + tm, LANES), BF16),
                        pltpu.VMEM((tm, ATTN_WIDTH), BF16),
                        pltpu.VMEM((tm, ATTN_WIDTH), BF16)],
        compiler_params=pltpu.CompilerParams(
            dimension_semantics=("arbitrary",), vmem_limit_bytes=VMEM_LIMIT),
        name="attention_merge",
    )(sinks, inv_freq_row, pos2, proj2, gh, *([proj2] * len(gate_specs)), x2d, wa, wh, wo, fgain.reshape(1, d))


def kernel(x, positions, norm_gain, w_in, attn_sinks, hgrn_lower_bounds, hgrn_norm_gain, w_attn_out,
           w_hgrn_out, w_o, final_norm_gain):
    b, t, d = x.shape
    depth = w_in.shape[0]
    assert w_in.shape[2] == IN_WIDTH and _SEG_OFF["ma"][1] == d
    half = ATTN_HEAD_DIM // 2
    inv_freq = ROPE_THETA ** (-jnp.arange(half, dtype=F32) / half)
    inv_freq_row = jnp.tile(inv_freq, LANES // half).reshape(1, LANES)
    pos2 = positions.astype(jnp.int32).reshape(b * t, 1)

    x2d = x.reshape(b * t, d)
    for l in range(depth):
        proj = _in_projection(x2d, norm_gain[l], w_in[l].astype(BF16))
        gh = _hgrn(proj.reshape(b, t, IN_WIDTH), hgrn_lower_bounds.astype(F32),
                   hgrn_norm_gain[l].reshape(1, HGRN_WIDTH).astype(F32), l)
        x2d = _attn_merge(proj, pos2, attn_sinks[l].astype(F32), inv_freq_row, gh.reshape(b * t, HGRN_WIDTH), x2d,
                          w_attn_out[l].astype(BF16), w_hgrn_out[l].astype(BF16), w_o[l].astype(BF16),
                          final_norm_gain, l == depth - 1, seq_len=t)
    return x2d.reshape(b, t, d)
```

```python
import functools

import numpy as np
import jax
import jax.numpy as jnp
from jax import lax
from jax.experimental import pallas as pl
from jax.experimental.pallas import tpu as pltpu

F32 = jnp.float32
BF16 = jnp.bfloat16

ATTN_HEADS = 16
ATTN_KV_HEADS = 4
ATTN_HEAD_DIM = 64
ATTN_GROUP = ATTN_HEADS // ATTN_KV_HEADS
ATTN_WIDTH = ATTN_HEADS * ATTN_HEAD_DIM
KV_WIDTH = ATTN_KV_HEADS * ATTN_HEAD_DIM
WINDOW = 128
ROPE_THETA = 10000.0
HGRN_HEADS = 8
HGRN_KEY_DIM = 128
HGRN_VALUE_DIM = 128
HGRN_WIDTH = HGRN_HEADS * HGRN_VALUE_DIM
NORM_EPS = 1e-6
LOG2_E = 1.4426950408889634

LANES = 128
VMEM_LIMIT = 56 * 1024 * 1024

_SEGMENTS = (("aq", 1024), ("ak", 256), ("av", 256), ("ag", 1024), ("hq", 1024),
             ("hf", 1024), ("hi", 1024), ("hg", 1024), ("ma", 2048), ("mh", 2048))
_SEG_OFF = {}
IN_WIDTH = 0
for _name, _w in _SEGMENTS:
    _SEG_OFF[_name] = (IN_WIDTH, _w)
    IN_WIDTH += _w
_COL_BLOCK = 512


def _segment_specs(name, lead_block, index_fn):
    off, width = _SEG_OFF[name]
    cb = min(_COL_BLOCK, width)
    assert off % cb == 0 and width % cb == 0
    return [pl.BlockSpec(lead_block + (cb,), lambda *ids, c=(off + i * cb) // cb: index_fn(*ids) + (c,))
            for i in range(width // cb)]


class _Cols:
    def __init__(self, refs):
        self._refs = refs
        self._cb = refs[0].shape[-1]

    def __getitem__(self, idx):
        rows, cols = idx
        start = 0 if cols.start is None else cols.start
        stop = self._cb * len(self._refs) if cols.stop is None else cols.stop
        pieces, c = [], start
        while c < stop:
            i = c // self._cb
            hi = min(stop, (i + 1) * self._cb)
            pieces.append(self._refs[i][rows, c - i * self._cb:hi - i * self._cb])
            c = hi
        return pieces[0] if len(pieces) == 1 else jnp.concatenate(pieces, axis=1)


def _sigmoid(x):
    return 1.0 / (1.0 + jnp.exp(-x))


def _inproj_kernel(x_ref, g_ref, w_ref, o_ref, h_ref, *, row_chunk):
    @pl.when(pl.program_id(1) == 0)
    def _():
        def body(r, carry):
            rows = pl.ds(pl.multiple_of(r * row_chunk, row_chunk), row_chunk)
            x = x_ref[rows, :]
            ms = jnp.mean(x * x, axis=-1, keepdims=True)
            h_ref[rows, :] = ((x * lax.rsqrt(ms + NORM_EPS)) * g_ref[...]).astype(BF16)
            return carry
        lax.fori_loop(0, x_ref.shape[0] // row_chunk, body, 0)

    o_ref[...] = jnp.dot(h_ref[...], w_ref[...], preferred_element_type=F32).astype(o_ref.dtype)


def _in_projection(x2d, gain, w_bf16, *, tm=1024, tn=1792):
    n_tok, d = x2d.shape
    n_out = w_bf16.shape[1]
    assert n_tok % tm == 0 and n_out % tn == 0
    return pl.pallas_call(
        functools.partial(_inproj_kernel, row_chunk=128),
        grid=(n_tok // tm, n_out // tn),
        in_specs=[pl.BlockSpec((tm, d), lambda i, j: (i, 0)),
                  pl.BlockSpec((1, d), lambda i, j: (0, 0)),
                  pl.BlockSpec((d, tn), lambda i, j: (0, j))],
        out_specs=pl.BlockSpec((tm, tn), lambda i, j: (i, j)),
        out_shape=jax.ShapeDtypeStruct((n_tok, n_out), BF16),
        scratch_shapes=[pltpu.VMEM((tm, d), BF16)],
        compiler_params=pltpu.CompilerParams(
            dimension_semantics=("arbitrary", "arbitrary"), vmem_limit_bytes=VMEM_LIMIT),
        name="in_projection",
    )(x2d, gain.reshape(1, d), w_bf16)


def _anchor_rows(b, m):
    c, w = b.shape
    g = c // m
    b3 = b.reshape(g, m, w)
    return jnp.broadcast_to(b3[:, m // 2 - 1:m // 2, :], (g, m, w)).reshape(c, w)


def _hgrn_kernel(lbp_ref, gain_ref, tril_ref, *refs, layer, chunk, n_chunks):
    ns = HGRN_WIDTH // _COL_BLOCK
    hq_ref, hf_ref, hi_ref, hg_ref = (_Cols(refs[i * ns:(i + 1) * ns]) for i in range(4))
    o_ref, st_ref, vt_ref = refs[4 * ns:]
    c = chunk
    dk, dv = HGRN_KEY_DIM, HGRN_VALUE_DIM
    w = HGRN_HEADS * dk

    @pl.when(pl.program_id(1) == 0)
    def _():
        st_ref[...] = jnp.zeros_like(st_ref)

    lbp = lbp_ref[...].astype(F32)
    e = jnp.exp(lbp - jnp.max(lbp, axis=0, keepdims=True))
    lb = jnp.sum(e[:layer + 1], axis=0, keepdims=True) / jnp.sum(e, axis=0, keepdims=True)

    ti = lax.broadcasted_iota(jnp.int32, (c, c), 0)
    si = lax.broadcasted_iota(jnp.int32, (c, c), 1)
    txs = ti ^ si
    causal = ti > si
    row = lax.broadcasted_iota(jnp.int32, (c, 1), 0)
    nt = (((1,), (1,)), ((), ()))
    chunks = [slice(ci * c, (ci + 1) * c) for ci in range(n_chunks)]
    heads = [slice(h * dk, (h + 1) * dk) for h in range(HGRN_HEADS)]

    st = {}
    for ci, rows in enumerate(chunks):
        k = (1.0 - lb) / (1.0 + jnp.exp(hf_ref[rows, :].astype(F32)))
        f = 1.0 - k
        log2_f = jnp.log(f) * LOG2_E
        hq = hq_ref[rows, :].astype(F32)
        hg = hg_ref[rows, :].astype(F32)
        f_hi = log2_f.astype(BF16)
        st[ci] = dict(f=f, k16=k.astype(BF16), q16=((hq / (1.0 + jnp.exp(-hq))) * (dk ** -0.5)).astype(BF16),
                      v=hi_ref[rows, :], gate=hg / (1.0 + jnp.exp(-hg)), f_hi=f_hi,
                      f_lo=(log2_f - f_hi.astype(F32)).astype(BF16))
    for ci in range(n_chunks):
        s = st[ci]
        cs = jnp.dot(tril_ref[...], jnp.concatenate([s["f_hi"], s["f_lo"]], axis=1), preferred_element_type=F32)
        s["b"] = cs[:, :w] + cs[:, w:]
    for ci in range(n_chunks):
        s = st[ci]
        s["a"] = [jnp.where(txs == 0, lax.dot_general(s["q16"][:, hc], s["k16"][:, hc], nt,
                                                      preferred_element_type=F32), 0.0) for hc in heads]
    m = c
    while m >= 2:
        half = m // 2
        sel = ((txs // half) == 1) & causal
        for ci in range(n_chunks):
            s = st[ci]
            b, f = s["b"], s["f"]
            if m > 4:
                z = jnp.exp2(-jnp.abs(b - _anchor_rows(b, m)))
            elif m == 4:
                u = row % 4
                z = jnp.where(u == 0, pltpu.roll(f, c - 1, axis=0),
                              jnp.where(u == 1, 1.0, jnp.where(u == 2, f, f * pltpu.roll(f, 1, axis=0))))
            else:
                z = jnp.where(row % 2 == 1, f, 1.0)
            z = z.astype(BF16)
            qz = s["q16"] * z
            kz = s["k16"] * z
            s["a"] = [jnp.where(sel, lax.dot_general(qz[:, hc], kz[:, hc], nt, preferred_element_type=F32), a)
                      for hc, a in zip(heads, s["a"])]
        m = half
    for ci in range(n_chunks):
        s = st[ci]
        b = s["b"]
        b_last = b[c - 1:c, :]
        s["q_dec"] = s["q16"] * jnp.exp2(b).astype(BF16)
        s["k_dec"] = s["k16"] * jnp.exp2(b_last - b).astype(BF16)
        s["decay_last"] = jnp.exp2(b_last)
        s["o_intra"] = [jnp.dot(a.astype(BF16), s["v"][:, hc], preferred_element_type=F32)
                        for hc, a in zip(heads, s["a"])]
        vt_ref[ci] = s["v"].T

    gain = gain_ref[...].astype(F32)
    for ci, rows in enumerate(chunks):
        s = st[ci]
        for h, hc in enumerate(heads):
            state = st_ref[h]
            o = s["o_intra"][h] + lax.dot_general(s["q_dec"][:, hc], state.astype(BF16), nt,
                                                  preferred_element_type=F32)
            st_ref[h] = state * s["decay_last"][:, hc] + jnp.dot(vt_ref[ci, hc, :], s["k_dec"][:, hc],
                                                                 preferred_element_type=F32)
            ms = jnp.mean(o * o, axis=-1, keepdims=True)
            y = (o * lax.rsqrt(ms + NORM_EPS)) * gain[:, hc]
            o_ref[rows, hc] = (y * s["gate"][:, hc]).astype(o_ref.dtype)


def _hgrn(proj3, lb_params, gain_row, layer, *, chunk=128, n_chunks=4):
    b, t, _ = proj3.shape
    rows = chunk * n_chunks
    assert t % rows == 0 and chunk & (chunk - 1) == 0
    w = HGRN_WIDTH
    tril = jnp.asarray(np.tril(np.ones((chunk, chunk), np.float32)), dtype=BF16)
    seg = lambda name: _segment_specs(name, (None, rows), lambda bi, i: (bi, i))
    in_specs = [pl.BlockSpec(lb_params.shape, lambda bi, i: (0, 0)),
                pl.BlockSpec((1, w), lambda bi, i: (0, 0)),
                pl.BlockSpec((chunk, chunk), lambda bi, i: (0, 0)),
                *seg("hq"), *seg("hf"), *seg("hi"), *seg("hg")]
    return pl.pallas_call(
        functools.partial(_hgrn_kernel, layer=layer, chunk=chunk, n_chunks=n_chunks),
        grid=(b, t // rows),
        in_specs=in_specs,
        out_specs=pl.BlockSpec((None, rows, w), lambda bi, i: (bi, i, 0)),
        out_shape=jax.ShapeDtypeStruct((b, t, w), BF16),
        scratch_shapes=[pltpu.VMEM((HGRN_HEADS, HGRN_VALUE_DIM, HGRN_KEY_DIM), F32),
                        pltpu.VMEM((n_chunks, w, chunk), BF16)],
        compiler_params=pltpu.CompilerParams(
            dimension_semantics=("arbitrary", "arbitrary"), vmem_limit_bytes=VMEM_LIMIT),
        name="hgrn2",
    )(lb_params, gain_row, tril, *([proj3] * (len(in_specs) - 3)))


def _rope_tables(pos_ref, inv_freq_row):
    n_freq = ATTN_HEAD_DIM // 2
    n_slab = LANES // n_freq
    slab = pos_ref.shape[0] // n_slab
    assert slab * n_slab == pos_ref.shape[0]
    group = lax.broadcasted_iota(jnp.int32, (1, LANES), 1) // n_freq
    packed = jnp.zeros((slab, LANES), F32)
    for i in range(n_slab):
        packed = jnp.where(group == i, pos_ref[i * slab:(i + 1) * slab, :].astype(F32), packed)
    ang = packed * inv_freq_row
    cos_p, sin_p = jnp.cos(ang), jnp.sin(ang)

    def spread(t, i):
        t = jnp.where(group == i, t, 0.0)
        t = t + pltpu.roll(t, n_freq, axis=1)
        return t + pltpu.roll(t, 2 * n_freq, axis=1)

    assert n_slab == 4
    cos = jnp.concatenate([spread(cos_p, i) for i in range(n_slab)], axis=0)
    sin = jnp.concatenate([spread(sin_p, i) for i in range(n_slab)], axis=0)
    lane = lax.broadcasted_iota(jnp.int32, (1, LANES), 1)
    first = (lane % ATTN_HEAD_DIM) < n_freq
    return cos, jnp.where(first, -sin, 0.0), jnp.where(first, 0.0, sin)


def _rope(x, cos, sin_first, sin_second):
    half = ATTN_HEAD_DIM // 2
    from_right = pltpu.roll(x, LANES - half, axis=1)
    from_left = pltpu.roll(x, half, axis=1)
    return x * cos + from_right * sin_first + from_left * sin_second


class _AttnBlock:
    def __init__(self, first_block, sink_ref, invf_ref, pos_ref, q_ref, g_ref, k_ref, v_ref, o_ref, kpad_ref,
                 vpad_ref, *, tq):
        self.first_block, self.sink_ref, self.invf_ref, self.pos_ref = first_block, sink_ref, invf_ref, pos_ref
        self.q_ref, self.g_ref, self.k_ref, self.v_ref, self.o_ref = q_ref, g_ref, k_ref, v_ref, o_ref
        self.kpad_ref, self.vpad_ref, self.tq = kpad_ref, vpad_ref, tq
        self.items = [(sub, g) for sub in range(tq // WINDOW) for g in range(ATTN_KV_HEADS)]
        self.lo_lane = lax.broadcasted_iota(jnp.int32, (1, LANES), 1) < ATTN_HEAD_DIM

    def prepare(self):
        blk, tq, kpad_ref, vpad_ref, lo_lane = WINDOW, self.tq, self.kpad_ref, self.vpad_ref, self.lo_lane

        @pl.when(self.first_block)
        def _():
            kpad_ref[:, :blk, :] = jnp.zeros((2 * ATTN_KV_HEADS, blk, LANES), BF16)
            vpad_ref[:, :blk, :] = jnp.zeros((2 * ATTN_KV_HEADS, blk, LANES), BF16)

        @pl.when(jnp.logical_not(self.first_block))
        def _():
            kpad_ref[:, :blk, :] = kpad_ref[:, tq:, :]
            vpad_ref[:, :blk, :] = vpad_ref[:, tq:, :]

        self.trig = _rope_tables(self.pos_ref, self.invf_ref[...])
        for s in range(KV_WIDTH // LANES):
            cols = slice(s * LANES, (s + 1) * LANES)
            kk = _rope(self.k_ref[:, cols].astype(F32), *self.trig)
            vv = self.v_ref[:, cols].astype(F32)
            kk_sw = pltpu.roll(kk, ATTN_HEAD_DIM, axis=1)
            vv_sw = pltpu.roll(vv, ATTN_HEAD_DIM, axis=1)
            for ref, own, swapped in ((kpad_ref, kk, kk_sw), (vpad_ref, vv, vv_sw)):
                ref[4 * s + 0, blk:, :] = jnp.where(lo_lane, own, 0.0).astype(BF16)
                ref[4 * s + 1, blk:, :] = jnp.where(lo_lane, 0.0, swapped).astype(BF16)
                ref[4 * s + 2, blk:, :] = jnp.where(lo_lane, swapped, 0.0).astype(BF16)
                ref[4 * s + 3, blk:, :] = jnp.where(lo_lane, 0.0, own).astype(BF16)

    def scores(self, item):
        sub, g = item
        blk = WINDOW
        rows = slice(sub * blk, (sub + 1) * blk)
        keys = slice(sub * blk, sub * blk + 2 * blk)
        scale = (ATTN_HEAD_DIM ** -0.5) * LOG2_E
        nt = (((1,), (1,)), ((), ()))
        q2 = jnp.concatenate(
            [_rope(self.q_ref[rows, p * LANES:(p + 1) * LANES].astype(F32), *(t[rows] for t in self.trig))
             for p in (2 * g, 2 * g + 1)], axis=0)
        q2 = (q2 * scale).astype(BF16)
        return [lax.dot_general(q2, self.kpad_ref[2 * g + which, keys, :], nt, preferred_element_type=F32)
                for which in range(2)]

    def softmax(self, item, s_pair):
        sub, g = item
        blk = WINDOW
        n_stack = ATTN_GROUP // 2
        r_idx = lax.broadcasted_iota(jnp.int32, (blk, 2 * blk), 0)
        c_idx = lax.broadcasted_iota(jnp.int32, (blk, 2 * blk), 1)
        mask = (c_idx > r_idx) & (c_idx <= r_idx + blk)
        if sub == 0:
            mask = mask & (c_idx >= jnp.where(self.first_block, blk, 0))
        mask = jnp.concatenate([mask] * n_stack, axis=0)
        lo_rows = lax.broadcasted_iota(jnp.int32, (n_stack * blk, 1), 0) < blk
        out = []
        for which, s in enumerate(s_pair):
            s = jnp.where(mask, s, -jnp.inf)
            sink = jnp.where(lo_rows, self.sink_ref[4 * g + which], self.sink_ref[4 * g + 2 + which]) * LOG2_E
            m = jnp.maximum(jnp.max(s, axis=-1, keepdims=True), sink)
            p = jnp.exp2(s - m)
            denom = jnp.sum(p, axis=-1, keepdims=True) + jnp.exp2(sink - m)
            out.append((p.astype(BF16), 1.0 / denom))
        return out

    def values(self, item, pw):
        sub, g = item
        keys = slice(sub * WINDOW, sub * WINDOW + 2 * WINDOW)
        acc = jnp.dot(pw[0][0], self.vpad_ref[2 * g, keys, :], preferred_element_type=F32)
        acc = acc + jnp.dot(pw[1][0], self.vpad_ref[2 * g + 1, keys, :], preferred_element_type=F32)
        return acc * jnp.where(self.lo_lane, pw[0][1], pw[1][1])

    def store(self, item, attn):
        sub, g = item
        blk = WINDOW
        rows = slice(sub * blk, (sub + 1) * blk)
        for i, p in enumerate((2 * g, 2 * g + 1)):
            cols = slice(p * LANES, (p + 1) * LANES)
            gate = self.g_ref[rows, cols].astype(F32)
            self.o_ref[rows, cols] = (attn[i * blk:(i + 1) * blk] * (gate / (1.0 + jnp.exp(-gate)))
                                      ).astype(self.o_ref.dtype)


def _attn_merge_kernel(sink_ref, invf_ref, pos_ref, *refs, tm, blocks_per_seq, final_norm):
    nq = ATTN_WIDTH // _COL_BLOCK
    q_ref, g_ref = _Cols(refs[:nq]), _Cols(refs[nq:2 * nq])
    k_ref, v_ref, gh_ref = refs[2 * nq:2 * nq + 3]
    rest = refs[2 * nq + 3:]
    ng = (len(rest) - 10) // 2
    ma_ref, mh_ref = _Cols(rest[:ng]), _Cols(rest[ng:2 * ng])
    x_ref, wa_ref, wh_ref, wo_ref, fg_ref, o_ref, kpad_ref, vpad_ref, ga_cur, ga_next = rest[2 * ng:]
    i = pl.program_id(0)
    first_block = lax.rem(i, blocks_per_seq) == 0

    @pl.when(i == 0)
    def _():
        ga_cur[...] = jnp.zeros_like(ga_cur)

    attn = _AttnBlock(first_block, sink_ref, invf_ref, pos_ref, q_ref, g_ref, k_ref, v_ref, ga_next, kpad_ref,
                      vpad_ref, tq=tm)
    items = attn.items
    half = len(items) // 2
    attn.prepare()
    scores = [attn.scores(it) for it in items]
    ya = jnp.dot(ga_cur[...], wa_ref[...], preferred_element_type=F32)
    probs = [attn.softmax(it, s) for it, s in zip(items[:half], scores[:half])]
    yh = jnp.dot(gh_ref[...], wh_ref[...], preferred_element_type=F32)
    probs += [attn.softmax(it, s) for it, s in zip(items[half:], scores[half:])]
    outs = [attn.values(it, pw) for it, pw in zip(items, probs)]
    merged = (_sigmoid(ma_ref[:, :].astype(F32)) * ya + _sigmoid(mh_ref[:, :].astype(F32)) * yh).astype(BF16)
    y = x_ref[...] + jnp.dot(merged, wo_ref[...], preferred_element_type=F32)
    for it, o in zip(items, outs):
        attn.store(it, o)
    if final_norm:
        ms = jnp.mean(y * y, axis=-1, keepdims=True)
        y = (y * lax.rsqrt(ms + NORM_EPS)) * fg_ref[...]
    o_ref[...] = y
    ga_cur[...] = ga_next[...]


def _attn_merge(proj2, pos2, sinks, inv_freq_row, gh, x2d, wa, wh, wo, fgain, final_norm, *, seq_len, tm=256):
    n_tok, d = x2d.shape
    assert n_tok % tm == 0 and seq_len % tm == 0 and tm % WINDOW == 0
    nb = n_tok // tm
    cur = lambda i: (jnp.minimum(i, nb - 1),)
    prv = lambda i: (jnp.maximum(i - 1, 0),)
    const = lambda shape: pl.BlockSpec(shape, lambda i: (0, 0), pipeline_mode=pl.Buffered(1))
    seg_c = lambda name: _segment_specs(name, (tm,), cur)
    seg_p = lambda name: _segment_specs(name, (tm,), prv)
    attn_specs = [*seg_c("aq"), *seg_c("ag"), *seg_c("ak"), *seg_c("av")]
    gate_specs = [*seg_p("ma"), *seg_p("mh")]
    in_specs = [pl.BlockSpec(memory_space=pltpu.SMEM),
                pl.BlockSpec((1, LANES), lambda i: (0, 0)),
                pl.BlockSpec((tm, 1), lambda i: cur(i) + (0,)),
                *attn_specs,
                pl.BlockSpec((tm, HGRN_WIDTH), lambda i: prv(i) + (0,)),
                *gate_specs,
                pl.BlockSpec((tm, d), lambda i: prv(i) + (0,)),
                const(wa.shape), const(wh.shape), const(wo.shape), const((1, d))]
    return pl.pallas_call(
        functools.partial(_attn_merge_kernel, tm=tm, blocks_per_seq=seq_len // tm, final_norm=final_norm),
        grid=(nb + 1,),
        in_specs=in_specs,
        out_specs=pl.BlockSpec((tm, d), lambda i: prv(i) + (0,)),
        out_shape=jax.ShapeDtypeStruct((n_tok, d), F32),
        scratch_shapes=[pltpu.VMEM((2 * ATTN_KV_HEADS, WINDOW + tm, LANES), BF16),
                        pltpu.VMEM((2 * ATTN_KV_HEADS, WINDOW + tm, LANES), BF16),
                        pltpu.VMEM((tm, ATTN_WIDTH), BF16),
                        pltpu.VMEM((tm, ATTN_WIDTH), BF16)],
        compiler_params=pltpu.CompilerParams(
            dimension_semantics=("arbitrary",), vmem_limit_bytes=VMEM_LIMIT),
        name="attention_merge",
    )(sinks, inv_freq_row, pos2, *([proj2] * len(attn_specs)), gh, *([proj2] * len(gate_specs)), x2d, wa, wh, wo,
      fgain.reshape(1, d))


def kernel(x, positions, norm_gain, w_in, attn_sinks, hgrn_lower_bounds, hgrn_norm_gain, w_attn_out,
           w_hgrn_out, w_o, final_norm_gain):
    b, t, d = x.shape
    depth = w_in.shape[0]
    assert w_in.shape[2] == IN_WIDTH and _SEG_OFF["ma"][1] == d
    half = ATTN_HEAD_DIM // 2
    inv_freq = ROPE_THETA ** (-jnp.arange(half, dtype=F32) / half)
    inv_freq_row = jnp.tile(inv_freq, LANES // half).reshape(1, LANES)
    pos2 = positions.astype(jnp.int32).reshape(b * t, 1)

    x2d = x.reshape(b * t, d)
    for l in range(depth):
        proj = _in_projection(x2d, norm_gain[l], w_in[l].astype(BF16))
        gh = _hgrn(proj.reshape(b, t, IN_WIDTH), hgrn_lower_bounds.astype(F32),
                   hgrn_norm_gain[l].reshape(1, HGRN_WIDTH).astype(F32), l)
        x2d = _attn_merge(proj, pos2, attn_sinks[l].astype(F32), inv_freq_row, gh.reshape(b * t, HGRN_WIDTH), x2d,
                          w_attn_out[l].astype(BF16), w_hgrn_out[l].astype(BF16), w_o[l].astype(BF16),
                          final_norm_gain, l == depth - 1, seq_len=t)
    return x2d.reshape(b, t, d)
```

```python
import functools

import numpy as np
import jax
import jax.numpy as jnp
from jax import lax
from jax.experimental import pallas as pl
from jax.experimental.pallas import tpu as pltpu

F32 = jnp.float32
BF16 = jnp.bfloat16

ATTN_HEADS = 16
ATTN_KV_HEADS = 4
ATTN_HEAD_DIM = 64
ATTN_GROUP = ATTN_HEADS // ATTN_KV_HEADS
ATTN_WIDTH = ATTN_HEADS * ATTN_HEAD_DIM
KV_WIDTH = ATTN_KV_HEADS * ATTN_HEAD_DIM
WINDOW = 128
ROPE_THETA = 10000.0
HGRN_HEADS = 8
HGRN_KEY_DIM = 128
HGRN_VALUE_DIM = 128
HGRN_WIDTH = HGRN_HEADS * HGRN_VALUE_DIM
NORM_EPS = 1e-6
LOG2_E = 1.4426950408889634

LANES = 128
VMEM_LIMIT = 56 * 1024 * 1024

_SEGMENTS = (("aq", 1024), ("ak", 256), ("av", 256), ("ag", 1024), ("hq", 1024),
             ("hf", 1024), ("hi", 1024), ("hg", 1024), ("ma", 2048), ("mh", 2048))
_SEG_OFF = {}
IN_WIDTH = 0
for _name, _w in _SEGMENTS:
    _SEG_OFF[_name] = (IN_WIDTH, _w)
    IN_WIDTH += _w
_COL_BLOCK = 512


def _segment_specs(name, lead_block, index_fn):
    off, width = _SEG_OFF[name]
    cb = min(_COL_BLOCK, width)
    assert off % cb == 0 and width % cb == 0
    return [pl.BlockSpec(lead_block + (cb,), lambda *ids, c=(off + i * cb) // cb: index_fn(*ids) + (c,))
            for i in range(width // cb)]


class _Cols:
    def __init__(self, refs):
        self._refs = refs
        self._cb = refs[0].shape[-1]

    def __getitem__(self, idx):
        rows, cols = idx
        start = 0 if cols.start is None else cols.start
        stop = self._cb * len(self._refs) if cols.stop is None else cols.stop
        pieces, c = [], start
        while c < stop:
            i = c // self._cb
            hi = min(stop, (i + 1) * self._cb)
            pieces.append(self._refs[i][rows, c - i * self._cb:hi - i * self._cb])
            c = hi
        return pieces[0] if len(pieces) == 1 else jnp.concatenate(pieces, axis=1)


def _sigmoid(x):
    return 1.0 / (1.0 + jnp.exp(-x))


def _inproj_kernel(x_ref, g_ref, w_ref, o_ref, h_ref, *, row_chunk):
    @pl.when(pl.program_id(1) == 0)
    def _():
        def body(r, carry):
            rows = pl.ds(pl.multiple_of(r * row_chunk, row_chunk), row_chunk)
            x = x_ref[rows, :]
            ms = jnp.mean(x * x, axis=-1, keepdims=True)
            h_ref[rows, :] = ((x * lax.rsqrt(ms + NORM_EPS)) * g_ref[...]).astype(BF16)
            return carry
        lax.fori_loop(0, x_ref.shape[0] // row_chunk, body, 0)

    o_ref[...] = jnp.dot(h_ref[...], w_ref[...], preferred_element_type=F32).astype(o_ref.dtype)


def _in_projection(x2d, gain, w_bf16, *, tm=1024, tn=1792):
    n_tok, d = x2d.shape
    n_out = w_bf16.shape[1]
    assert n_tok % tm == 0 and n_out % tn == 0
    return pl.pallas_call(
        functools.partial(_inproj_kernel, row_chunk=128),
        grid=(n_tok // tm, n_out // tn),
        in_specs=[pl.BlockSpec((tm, d), lambda i, j: (i, 0)),
                  pl.BlockSpec((1, d), lambda i, j: (0, 0)),
                  pl.BlockSpec((d, tn), lambda i, j: (0, j))],
        out_specs=pl.BlockSpec((tm, tn), lambda i, j: (i, j)),
        out_shape=jax.ShapeDtypeStruct((n_tok, n_out), BF16),
        scratch_shapes=[pltpu.VMEM((tm, d), BF16)],
        compiler_params=pltpu.CompilerParams(
            dimension_semantics=("arbitrary", "arbitrary"), vmem_limit_bytes=VMEM_LIMIT),
        name="in_projection",
    )(x2d, gain.reshape(1, d), w_bf16)


def _anchor_rows(b, m):
    c, w = b.shape
    g = c // m
    b3 = b.reshape(g, m, w)
    return jnp.broadcast_to(b3[:, m // 2 - 1:m // 2, :], (g, m, w)).reshape(c, w)


_RING_SLOTS = 3


class _RingCols:
    def __init__(self, ring_ref, slot, off):
        self._ring, self._slot, self._off = ring_ref, slot, off

    def __getitem__(self, idx):
        rows, cols = idx
        start = 0 if cols.start is None else cols.start
        stop = HGRN_WIDTH if cols.stop is None else cols.stop
        return self._ring[self._slot, rows, self._off + start:self._off + stop]


def _hgrn_kernel(lbp_ref, gain_ref, tril_ref, proj_hbm, o_ref, st_ref, vt_ref, ring_ref, sem,
                 *, layer, chunk, n_chunks, col0):
    c = chunk
    dk, dv = HGRN_KEY_DIM, HGRN_VALUE_DIM
    w = HGRN_HEADS * dk
    steps_per_seq = pl.num_programs(1)
    n_steps = pl.num_programs(0) * steps_per_seq
    step = pl.program_id(0) * steps_per_seq + pl.program_id(1)

    def block_copy(s):
        slot = lax.rem(s, _RING_SLOTS)
        src = proj_hbm.at[lax.div(s, steps_per_seq), pl.ds(lax.rem(s, steps_per_seq) * (c * n_chunks), c * n_chunks),
                          pl.ds(col0, 4 * w)]
        return pltpu.make_async_copy(src, ring_ref.at[slot], sem.at[slot])

    @pl.when(step == 0)
    def _():
        block_copy(step).start()

    @pl.when((step == 0) & (n_steps > 1))
    def _():
        block_copy(step + 1).start()

    @pl.when(step + 2 < n_steps)
    def _():
        block_copy(step + 2).start()

    block_copy(step).wait()
    slot = lax.rem(step, _RING_SLOTS)
    hq_ref, hf_ref, hi_ref, hg_ref = (_RingCols(ring_ref, slot, j * w) for j in range(4))

    @pl.when(pl.program_id(1) == 0)
    def _():
        st_ref[...] = jnp.zeros_like(st_ref)

    lbp = lbp_ref[...].astype(F32)
    e = jnp.exp(lbp - jnp.max(lbp, axis=0, keepdims=True))
    lb = jnp.sum(e[:layer + 1], axis=0, keepdims=True) / jnp.sum(e, axis=0, keepdims=True)

    ti = lax.broadcasted_iota(jnp.int32, (c, c), 0)
    si = lax.broadcasted_iota(jnp.int32, (c, c), 1)
    txs = ti ^ si
    causal = ti > si
    row = lax.broadcasted_iota(jnp.int32, (c, 1), 0)
    nt = (((1,), (1,)), ((), ()))
    chunks = [slice(ci * c, (ci + 1) * c) for ci in range(n_chunks)]
    heads = [slice(h * dk, (h + 1) * dk) for h in range(HGRN_HEADS)]

    st = {}
    for ci, rows in enumerate(chunks):
        k = (1.0 - lb) / (1.0 + jnp.exp(hf_ref[rows, :].astype(F32)))
        f = 1.0 - k
        log2_f = jnp.log(f) * LOG2_E
        hq = hq_ref[rows, :].astype(F32)
        hg = hg_ref[rows, :].astype(F32)
        f_hi = log2_f.astype(BF16)
        st[ci] = dict(f=f, k16=k.astype(BF16), q16=((hq / (1.0 + jnp.exp(-hq))) * (dk ** -0.5)).astype(BF16),
                      v=hi_ref[rows, :], gate=hg / (1.0 + jnp.exp(-hg)), f_hi=f_hi,
                      f_lo=(log2_f - f_hi.astype(F32)).astype(BF16))
    for ci in range(n_chunks):
        s = st[ci]
        cs = jnp.dot(tril_ref[...], jnp.concatenate([s["f_hi"], s["f_lo"]], axis=1), preferred_element_type=F32)
        s["b"] = cs[:, :w] + cs[:, w:]
    for ci in range(n_chunks):
        s = st[ci]
        s["a"] = [jnp.where(txs == 0, lax.dot_general(s["q16"][:, hc], s["k16"][:, hc], nt,
                                                      preferred_element_type=F32), 0.0) for hc in heads]
    m = c
    while m >= 2:
        half = m // 2
        sel = ((txs // half) == 1) & causal
        for ci in range(n_chunks):
            s = st[ci]
            b, f = s["b"], s["f"]
            if m > 4:
                z = jnp.exp2(-jnp.abs(b - _anchor_rows(b, m)))
            elif m == 4:
                u = row % 4
                z = jnp.where(u == 0, pltpu.roll(f, c - 1, axis=0),
                              jnp.where(u == 1, 1.0, jnp.where(u == 2, f, f * pltpu.roll(f, 1, axis=0))))
            else:
                z = jnp.where(row % 2 == 1, f, 1.0)
            z = z.astype(BF16)
            qz = s["q16"] * z
            kz = s["k16"] * z
            s["a"] = [jnp.where(sel, lax.dot_general(qz[:, hc], kz[:, hc], nt, preferred_element_type=F32), a)
                      for hc, a in zip(heads, s["a"])]
        m = half
    for ci in range(n_chunks):
        s = st[ci]
        b = s["b"]
        b_last = b[c - 1:c, :]
        s["q_dec"] = s["q16"] * jnp.exp2(b).astype(BF16)
        s["k_dec"] = s["k16"] * jnp.exp2(b_last - b).astype(BF16)
        s["decay_last"] = jnp.exp2(b_last)
        s["o_intra"] = [jnp.dot(a.astype(BF16), s["v"][:, hc], preferred_element_type=F32)
                        for hc, a in zip(heads, s["a"])]
        vt_ref[ci] = s["v"].T

    gain = gain_ref[...].astype(F32)
    for ci, rows in enumerate(chunks):
        s = st[ci]
        for h, hc in enumerate(heads):
            state = st_ref[h]
            o = s["o_intra"][h] + lax.dot_general(s["q_dec"][:, hc], state.astype(BF16), nt,
                                                  preferred_element_type=F32)
            st_ref[h] = state * s["decay_last"][:, hc] + jnp.dot(vt_ref[ci, hc, :], s["k_dec"][:, hc],
                                                                 preferred_element_type=F32)
            ms = jnp.mean(o * o, axis=-1, keepdims=True)
            y = (o * lax.rsqrt(ms + NORM_EPS)) * gain[:, hc]
            o_ref[rows, hc] = (y * s["gate"][:, hc]).astype(o_ref.dtype)


def _hgrn(proj3, lb_params, gain_row, layer, *, chunk=128, n_chunks=4):
    b, t, _ = proj3.shape
    rows = chunk * n_chunks
    assert t % rows == 0 and chunk & (chunk - 1) == 0
    w = HGRN_WIDTH
    tril = jnp.asarray(np.tril(np.ones((chunk, chunk), np.float32)), dtype=BF16)
    col0 = _SEG_OFF["hq"][0]
    assert [_SEG_OFF[n] for n in ("hq", "hf", "hi", "hg")] == [(col0 + j * w, w) for j in range(4)]
    return pl.pallas_call(
        functools.partial(_hgrn_kernel, layer=layer, chunk=chunk, n_chunks=n_chunks, col0=col0),
        grid=(b, t // rows),
        in_specs=[pl.BlockSpec(lb_params.shape, lambda bi, i: (0, 0)),
                  pl.BlockSpec((1, w), lambda bi, i: (0, 0)),
                  pl.BlockSpec((chunk, chunk), lambda bi, i: (0, 0)),
                  pl.BlockSpec(memory_space=pl.ANY)],
        out_specs=pl.BlockSpec((None, rows, w), lambda bi, i: (bi, i, 0)),
        out_shape=jax.ShapeDtypeStruct((b, t, w), BF16),
        scratch_shapes=[pltpu.VMEM((HGRN_HEADS, HGRN_VALUE_DIM, HGRN_KEY_DIM), F32),
                        pltpu.VMEM((n_chunks, w, chunk), BF16),
                        pltpu.VMEM((_RING_SLOTS, rows, 4 * w), BF16),
                        pltpu.SemaphoreType.DMA((_RING_SLOTS,))],
        compiler_params=pltpu.CompilerParams(
            dimension_semantics=("arbitrary", "arbitrary"), vmem_limit_bytes=VMEM_LIMIT),
        name="hgrn2",
    )(lb_params, gain_row, tril, proj3)


def _rope_tables(pos_ref, inv_freq_row):
    n_freq = ATTN_HEAD_DIM // 2
    n_slab = LANES // n_freq
    slab = pos_ref.shape[0] // n_slab
    assert slab * n_slab == pos_ref.shape[0]
    group = lax.broadcasted_iota(jnp.int32, (1, LANES), 1) // n_freq
    packed = jnp.zeros((slab, LANES), F32)
    for i in range(n_slab):
        packed = jnp.where(group == i, pos_ref[i * slab:(i + 1) * slab, :].astype(F32), packed)
    ang = packed * inv_freq_row
    cos_p, sin_p = jnp.cos(ang), jnp.sin(ang)

    def spread(t, i):
        t = jnp.where(group == i, t, 0.0)
        t = t + pltpu.roll(t, n_freq, axis=1)
        return t + pltpu.roll(t, 2 * n_freq, axis=1)

    assert n_slab == 4
    cos = jnp.concatenate([spread(cos_p, i) for i in range(n_slab)], axis=0)
    sin = jnp.concatenate([spread(sin_p, i) for i in range(n_slab)], axis=0)
    lane = lax.broadcasted_iota(jnp.int32, (1, LANES), 1)
    first = (lane % ATTN_HEAD_DIM) < n_freq
    return cos, jnp.where(first, -sin, 0.0), jnp.where(first, 0.0, sin)


def _rope(x, cos, sin_first, sin_second):
    half = ATTN_HEAD_DIM // 2
    from_right = pltpu.roll(x, LANES - half, axis=1)
    from_left = pltpu.roll(x, half, axis=1)
    return x * cos + from_right * sin_first + from_left * sin_second


class _AttnBlock:
    def __init__(self, first_block, sink_ref, invf_ref, pos_ref, q_ref, g_ref, k_ref, v_ref, o_ref, kpad_ref,
                 vpad_ref, *, tq):
        self.first_block, self.sink_ref, self.invf_ref, self.pos_ref = first_block, sink_ref, invf_ref, pos_ref
        self.q_ref, self.g_ref, self.k_ref, self.v_ref, self.o_ref = q_ref, g_ref, k_ref, v_ref, o_ref
        self.kpad_ref, self.vpad_ref, self.tq = kpad_ref, vpad_ref, tq
        self.items = [(sub, g) for sub in range(tq // WINDOW) for g in range(ATTN_KV_HEADS)]
        self.lo_lane = lax.broadcasted_iota(jnp.int32, (1, LANES), 1) < ATTN_HEAD_DIM

    def prepare(self):
        blk, tq, kpad_ref, vpad_ref, lo_lane = WINDOW, self.tq, self.kpad_ref, self.vpad_ref, self.lo_lane

        @pl.when(self.first_block)
        def _():
            kpad_ref[:, :blk, :] = jnp.zeros((2 * ATTN_KV_HEADS, blk, LANES), BF16)
            vpad_ref[:, :blk, :] = jnp.zeros((2 * ATTN_KV_HEADS, blk, LANES), BF16)

        @pl.when(jnp.logical_not(self.first_block))
        def _():
            kpad_ref[:, :blk, :] = kpad_ref[:, tq:, :]
            vpad_ref[:, :blk, :] = vpad_ref[:, tq:, :]

        self.trig = _rope_tables(self.pos_ref, self.invf_ref[...])
        for s in range(KV_WIDTH // LANES):
            cols = slice(s * LANES, (s + 1) * LANES)
            kk = _rope(self.k_ref[:, cols].astype(F32), *self.trig)
            vv = self.v_ref[:, cols].astype(F32)
            kk_sw = pltpu.roll(kk, ATTN_HEAD_DIM, axis=1)
            vv_sw = pltpu.roll(vv, ATTN_HEAD_DIM, axis=1)
            for ref, own, swapped in ((kpad_ref, kk, kk_sw), (vpad_ref, vv, vv_sw)):
                ref[4 * s + 0, blk:, :] = jnp.where(lo_lane, own, 0.0).astype(BF16)
                ref[4 * s + 1, blk:, :] = jnp.where(lo_lane, 0.0, swapped).astype(BF16)
                ref[4 * s + 2, blk:, :] = jnp.where(lo_lane, swapped, 0.0).astype(BF16)
                ref[4 * s + 3, blk:, :] = jnp.where(lo_lane, 0.0, own).astype(BF16)

    def scores(self, item):
        sub, g = item
        blk = WINDOW
        rows = slice(sub * blk, (sub + 1) * blk)
        keys = slice(sub * blk, sub * blk + 2 * blk)
        scale = (ATTN_HEAD_DIM ** -0.5) * LOG2_E
        nt = (((1,), (1,)), ((), ()))
        q2 = jnp.concatenate(
            [_rope(self.q_ref[rows, p * LANES:(p + 1) * LANES].astype(F32), *(t[rows] for t in self.trig))
             for p in (2 * g, 2 * g + 1)], axis=0)
        q2 = (q2 * scale).astype(BF16)
        return [lax.dot_general(q2, self.kpad_ref[2 * g + which, keys, :], nt, preferred_element_type=F32)
                for which in range(2)]

    def softmax(self, item, s_pair):
        sub, g = item
        blk = WINDOW
        n_stack = ATTN_GROUP // 2
        r_idx = lax.broadcasted_iota(jnp.int32, (blk, 2 * blk), 0)
        c_idx = lax.broadcasted_iota(jnp.int32, (blk, 2 * blk), 1)
        mask = (c_idx > r_idx) & (c_idx <= r_idx + blk)
        if sub == 0:
            mask = mask & (c_idx >= jnp.where(self.first_block, blk, 0))
        mask = jnp.concatenate([mask] * n_stack, axis=0)
        lo_rows = lax.broadcasted_iota(jnp.int32, (n_stack * blk, 1), 0) < blk
        out = []
        for which, s in enumerate(s_pair):
            s = jnp.where(mask, s, -jnp.inf)
            sink = jnp.where(lo_rows, self.sink_ref[4 * g + which], self.sink_ref[4 * g + 2 + which]) * LOG2_E
            m = jnp.maximum(jnp.max(s, axis=-1, keepdims=True), sink)
            p = jnp.exp2(s - m)
            denom = jnp.sum(p, axis=-1, keepdims=True) + jnp.exp2(sink - m)
            out.append((p.astype(BF16), 1.0 / denom))
        return out

    def values(self, item, pw):
        sub, g = item
        keys = slice(sub * WINDOW, sub * WINDOW + 2 * WINDOW)
        acc = jnp.dot(pw[0][0], self.vpad_ref[2 * g, keys, :], preferred_element_type=F32)
        acc = acc + jnp.dot(pw[1][0], self.vpad_ref[2 * g + 1, keys, :], preferred_element_type=F32)
        return acc * jnp.where(self.lo_lane, pw[0][1], pw[1][1])

    def store(self, item, attn):
        sub, g = item
        blk = WINDOW
        rows = slice(sub * blk, (sub + 1) * blk)
        for i, p in enumerate((2 * g, 2 * g + 1)):
            cols = slice(p * LANES, (p + 1) * LANES)
            gate = self.g_ref[rows, cols].astype(F32)
            self.o_ref[rows, cols] = (attn[i * blk:(i + 1) * blk] * (gate / (1.0 + jnp.exp(-gate)))
                                      ).astype(self.o_ref.dtype)


def _attn_merge_kernel(sink_ref, invf_ref, pos_ref, *refs, tm, blocks_per_seq, final_norm):
    nq = ATTN_WIDTH // _COL_BLOCK
    q_ref, g_ref = _Cols(refs[:nq]), _Cols(refs[nq:2 * nq])
    k_ref, v_ref, gh_ref = refs[2 * nq:2 * nq + 3]
    rest = refs[2 * nq + 3:]
    ng = (len(rest) - 10) // 2
    ma_ref, mh_ref = _Cols(rest[:ng]), _Cols(rest[ng:2 * ng])
    x_ref, wa_ref, wh_ref, wo_ref, fg_ref, o_ref, kpad_ref, vpad_ref, ga_cur, ga_next = rest[2 * ng:]
    i = pl.program_id(0)
    first_block = lax.rem(i, blocks_per_seq) == 0

    @pl.when(i == 0)
    def _():
        ga_cur[...] = jnp.zeros_like(ga_cur)

    attn = _AttnBlock(first_block, sink_ref, invf_ref, pos_ref, q_ref, g_ref, k_ref, v_ref, ga_next, kpad_ref,
                      vpad_ref, tq=tm)
    items = attn.items
    half = len(items) // 2
    attn.prepare()
    scores = [attn.scores(it) for it in items]
    ya = jnp.dot(ga_cur[...], wa_ref[...], preferred_element_type=F32)
    probs = [attn.softmax(it, s) for it, s in zip(items[:half], scores[:half])]
    yh = jnp.dot(gh_ref[...], wh_ref[...], preferred_element_type=F32)
    probs += [attn.softmax(it, s) for it, s in zip(items[half:], scores[half:])]
    outs = [attn.values(it, pw) for it, pw in zip(items, probs)]
    merged = (_sigmoid(ma_ref[:, :].astype(F32)) * ya + _sigmoid(mh_ref[:, :].astype(F32)) * yh).astype(BF16)
    y = x_ref[...] + jnp.dot(merged, wo_ref[...], preferred_element_type=F32)
    for it, o in zip(items, outs):
        attn.store(it, o)
    if final_norm:
        ms = jnp.mean(y * y, axis=-1, keepdims=True)
        y = (y * lax.rsqrt(ms + NORM_EPS)) * fg_ref[...]
    o_ref[...] = y
    ga_cur[...] = ga_next[...]


def _attn_merge(proj2, pos2, sinks, inv_freq_row, gh, x2d, wa, wh, wo, fgain, final_norm, *, seq_len, tm=256):
    n_tok, d = x2d.shape
    assert n_tok % tm == 0 and seq_len % tm == 0 and tm % WINDOW == 0
    nb = n_tok // tm
    cur = lambda i: (jnp.minimum(i, nb - 1),)
    prv = lambda i: (jnp.maximum(i - 1, 0),)
    const = lambda shape: pl.BlockSpec(shape, lambda i: (0, 0), pipeline_mode=pl.Buffered(1))
    seg_c = lambda name: _segment_specs(name, (tm,), cur)
    seg_p = lambda name: _segment_specs(name, (tm,), prv)
    attn_specs = [*seg_c("aq"), *seg_c("ag"), *seg_c("ak"), *seg_c("av")]
    gate_specs = [*seg_p("ma"), *seg_p("mh")]
    in_specs = [pl.BlockSpec(memory_space=pltpu.SMEM),
                pl.BlockSpec((1, LANES), lambda i: (0, 0)),
                pl.BlockSpec((tm, 1), lambda i: cur(i) + (0,)),
                *attn_specs,
                pl.BlockSpec((tm, HGRN_WIDTH), lambda i: prv(i) + (0,)),
                *gate_specs,
                pl.BlockSpec((tm, d), lambda i: prv(i) + (0,)),
                const(wa.shape), const(wh.shape), const(wo.shape), const((1, d))]
    return pl.pallas_call(
        functools.partial(_attn_merge_kernel, tm=tm, blocks_per_seq=seq_len // tm, final_norm=final_norm),
        grid=(nb + 1,),
        in_specs=in_specs,
        out_specs=pl.BlockSpec((tm, d), lambda i: prv(i) + (0,)),
        out_shape=jax.ShapeDtypeStruct((n_tok, d), F32),
        scratch_shapes=[pltpu.VMEM((2 * ATTN_KV_HEADS, WINDOW + tm, LANES), BF16),
                        pltpu.VMEM((2 * ATTN_KV_HEADS, WINDOW + tm, LANES), BF16),
                        pltpu.VMEM((tm, ATTN_WIDTH), BF16),
                        pltpu.VMEM((tm, ATTN_WIDTH), BF16)],
        compiler_params=pltpu.CompilerParams(
            dimension_semantics=("arbitrary",), vmem_limit_bytes=VMEM_LIMIT),
        name="attention_merge",
    )(sinks, inv_freq_row, pos2, *([proj2] * len(attn_specs)), gh, *([proj2] * len(gate_specs)), x2d, wa, wh, wo,
      fgain.reshape(1, d))


def kernel(x, positions, norm_gain, w_in, attn_sinks, hgrn_lower_bounds, hgrn_norm_gain, w_attn_out,
           w_hgrn_out, w_o, final_norm_gain):
    b, t, d = x.shape
    depth = w_in.shape[0]
    assert w_in.shape[2] == IN_WIDTH and _SEG_OFF["ma"][1] == d
    half = ATTN_HEAD_DIM // 2
    inv_freq = ROPE_THETA ** (-jnp.arange(half, dtype=F32) / half)
    inv_freq_row = jnp.tile(inv_freq, LANES // half).reshape(1, LANES)
    pos2 = positions.astype(jnp.int32).reshape(b * t, 1)

    x2d = x.reshape(b * t, d)
    for l in range(depth):
        proj = _in_projection(x2d, norm_gain[l], w_in[l].astype(BF16))
        gh = _hgrn(proj.reshape(b, t, IN_WIDTH), hgrn_lower_bounds.astype(F32),
                   hgrn_norm_gain[l].reshape(1, HGRN_WIDTH).astype(F32), l)
        x2d = _attn_merge(proj, pos2, attn_sinks[l].astype(F32), inv_freq_row, gh.reshape(b * t, HGRN_WIDTH), x2d,
                          w_attn_out[l].astype(BF16), w_hgrn_out[l].astype(BF16), w_o[l].astype(BF16),
                          final_norm_gain, l == depth - 1, seq_len=t)
    return x2d.reshape(b, t, d)
```
